```python
import math
import jax, jax.numpy as jnp
from jax import lax
import numpy as np

D_MODEL = 1024
BATCH = 2
SEQ = 16384
DEPTH = 2

CHUNK = 64
N_MIXERS = 2
RMS_EPS = 1e-6

SSD_EXPAND = 2
SSD_D_INNER = SSD_EXPAND * D_MODEL
SSD_HEAD_DIM = 64
SSD_N_HEADS = SSD_D_INNER // SSD_HEAD_DIM
SSD_N_GROUPS = 8
SSD_HEADS_PER_GROUP = SSD_N_HEADS // SSD_N_GROUPS
SSD_D_STATE = 128
SSD_CONV = 4
SSD_CHUNK = CHUNK
SSD_CONV_DIM = SSD_D_INNER + 2 * SSD_N_GROUPS * SSD_D_STATE
SSD_IN_DIM = SSD_D_INNER + SSD_CONV_DIM + SSD_N_HEADS
DT_MIN = 0.001
DT_MAX = 0.1

DA_HEAD_DIM = 64
DA_N_HEADS = D_MODEL // (2 * DA_HEAD_DIM)
DA_Q_BLOCK = 128

PEER_N_KEYS = 128
PEER_N_EXPERTS = PEER_N_KEYS * PEER_N_KEYS
PEER_HEADS = 8
PEER_TOPK = 16
PEER_D_KEY = 256
PEER_HALF = PEER_D_KEY // 2
PEER_TOKEN_BLOCK = 128

N_SSD_LAYERS = (DEPTH + N_MIXERS - 1) // N_MIXERS
N_ATTN_LAYERS = DEPTH // N_MIXERS

kernel_name = 'hybrid_ssd_diffattn_peer_block'


def rms_norm(x, w):
    xf = x.astype(jnp.float32)
    y = xf * lax.rsqrt(jnp.mean(xf * xf, axis=-1, keepdims=True) + RMS_EPS)
    return (y * w.astype(jnp.float32)).astype(x.dtype)


def causal_depthwise_conv(x, w, b):
    c = x.shape[-1]
    y = lax.conv_general_dilated(x, w[:, None, :].astype(x.dtype), window_strides=(1,),
                                 padding=[(SSD_CONV - 1, 0)],
                                 dimension_numbers=('NWC', 'WIO', 'NWC'),
                                 feature_group_count=c)
    return y + b.astype(x.dtype)


def ssd_chunked_scan(xh, dt, a, bmat, cmat):
    f32 = jnp.float32
    bsz, seq = xh.shape[0], xh.shape[1]
    nc = seq // SSD_CHUNK
    xd = xh.astype(f32) * dt[..., None]
    adt = dt * a

    def to_chunks(t):
        t = t.reshape((bsz, nc, SSD_CHUNK) + t.shape[2:])
        return jnp.moveaxis(t, 1, 0)

    xs = (to_chunks(xd), to_chunks(adt), to_chunks(bmat.astype(f32)), to_chunks(cmat.astype(f32)))
    causal = jnp.tril(jnp.ones((SSD_CHUNK, SSD_CHUNK), dtype=bool))

    def step(state, inp):
        x_c, a_c, b_c, c_c = inp
        a_cs = jnp.cumsum(a_c, axis=1)
        seg = a_cs[:, :, None] - a_cs[:, None, :]
        decay = jnp.exp(jnp.where(causal[None, :, :, None, None], seg, -jnp.inf))
        cb = jnp.einsum('blgn,bsgn->blsg', c_c, b_c)
        y_diag = jnp.einsum('blsg,blsge,bsgep->blgep', cb, decay, x_c)
        y_off = jnp.einsum('blgn,bgepn,blge->blgep', c_c, state, jnp.exp(a_cs))
        a_last = a_cs[:, -1]
        to_end = jnp.exp(a_last[:, None] - a_cs)
        new_state = (state * jnp.exp(a_last)[..., None, None]
                     + jnp.einsum('bsgn,bsge,bsgep->bgepn', b_c, to_end, x_c))
        return new_state, y_diag + y_off

    state0 = jnp.zeros((bsz, SSD_N_GROUPS, SSD_HEADS_PER_GROUP, SSD_HEAD_DIM, SSD_D_STATE), f32)
    _, ys = lax.scan(step, state0, xs)
    return jnp.moveaxis(ys, 0, 1).reshape(bsz, seq, SSD_N_GROUPS, SSD_HEADS_PER_GROUP, SSD_HEAD_DIM)


def ssd_mixer(h, in_w, conv_w, conv_b, dt_bias, a_log, d_skip, norm_w, out_w):
    f32 = jnp.float32
    bsz, seq, _ = h.shape
    g, e, p, n = SSD_N_GROUPS, SSD_HEADS_PER_GROUP, SSD_HEAD_DIM, SSD_D_STATE
    zxbcdt = h @ in_w
    z, xbc, dt_raw = jnp.split(zxbcdt, [SSD_D_INNER, SSD_D_INNER + SSD_CONV_DIM], axis=-1)
    xbc = jax.nn.silu(causal_depthwise_conv(xbc, conv_w, conv_b))
    xs, bmat, cmat = jnp.split(xbc, [SSD_D_INNER, SSD_D_INNER + g * n], axis=-1)
    xh = xs.reshape(bsz, seq, g, e, p)
    bmat = bmat.reshape(bsz, seq, g, n)
    cmat = cmat.reshape(bsz, seq, g, n)
    dt = jax.nn.softplus((dt_raw + dt_bias).astype(f32)).reshape(bsz, seq, g, e)
    a = -jnp.exp(a_log.astype(f32)).reshape(g, e)
    y = ssd_chunked_scan(xh, dt, a, bmat, cmat)
    y = y + xh.astype(f32) * d_skip.astype(f32).reshape(g, e)[:, :, None]
    y = y.reshape(bsz, seq, SSD_D_INNER) * jax.nn.silu(z.astype(f32))
    yg = y.reshape(bsz, seq, g, e * p)
    yg = yg * lax.rsqrt(jnp.mean(yg * yg, axis=-1, keepdims=True) + RMS_EPS)
    y = (yg.reshape(bsz, seq, SSD_D_INNER) * norm_w.astype(f32)).astype(h.dtype)
    return y @ out_w


def diff_attention(h, qkv_w, q_norm_w, k_norm_w, lam_q1, lam_k1, lam_q2, lam_k2,
                   subln_w, out_w, lambda_init):
    f32 = jnp.float32
    bsz, seq, dm = h.shape
    nh, d = DA_N_HEADS, DA_HEAD_DIM
    q, k, v = jnp.split(h @ qkv_w, 3, axis=-1)
    q = rms_norm(q.reshape(bsz, seq, nh, 2, d), q_norm_w)
    k = rms_norm(k.reshape(bsz, seq, nh, 2, d), k_norm_w)
    v = v.reshape(bsz, seq, nh, 2 * d)
    lam = (jnp.exp(jnp.sum((lam_q1 * lam_k1).astype(f32)))
           - jnp.exp(jnp.sum((lam_q2 * lam_k2).astype(f32))) + lambda_init)
    scale = d ** -0.5
    nb = seq // DA_Q_BLOCK
    qb = jnp.moveaxis(q.reshape(bsz, nb, DA_Q_BLOCK, nh, 2, d), 1, 0)
    key_chunk = jnp.arange(seq) // CHUNK

    def block(args):
        q_blk, blk_idx = args
        q_chunk = (blk_idx * DA_Q_BLOCK + jnp.arange(DA_Q_BLOCK)) // CHUNK
        mask = key_chunk[None, :] <= q_chunk[:, None]
        s = jnp.einsum('bqhmd,bkhmd->bhmqk', q_blk, k).astype(f32) * scale
        s = jnp.where(mask, s, -jnp.inf)
        pr = jax.nn.softmax(s, axis=-1)
        attn = pr[:, :, 0] - lam * pr[:, :, 1]
        return jnp.einsum('bhqk,bkhe->bqhe', attn.astype(v.dtype), v)

    o = lax.map(block, (qb, jnp.arange(nb)))
    o = jnp.moveaxis(o, 0, 1).reshape(bsz, seq, nh, 2 * d)
    o = rms_norm(o, subln_w) * (1.0 - lambda_init)
    return o.reshape(bsz, seq, dm) @ out_w


def peer_ffn(h, query_w, sub_keys, expert_u, expert_v):
    f32 = jnp.float32
    bsz, seq, dm = h.shape
    nb = (bsz * seq) // PEER_TOKEN_BLOCK
    tokens = h.reshape(nb, PEER_TOKEN_BLOCK, dm)
    kk = PEER_TOPK

    def block(xt):
        q = (xt @ query_w).reshape(-1, PEER_HEADS, 2, PEER_HALF)
        s = jnp.einsum('thcd,hcnd->thcn', q, sub_keys).astype(f32)
        top_s, top_i = lax.top_k(s, kk)
        cand_s = (top_s[:, :, 0, :, None] + top_s[:, :, 1, None, :]).reshape(-1, PEER_HEADS, kk * kk)
        cand_i = (top_i[:, :, 0, :, None] * PEER_N_KEYS + top_i[:, :, 1, None, :]).reshape(-1, PEER_HEADS, kk * kk)
        best_s, pos = lax.top_k(cand_s, kk)
        idx = jnp.take_along_axis(cand_i, pos, axis=-1)
        gate = jax.nn.softmax(best_s, axis=-1)
        u = expert_u[idx]
        vv = expert_v[idx]
        act = jax.nn.gelu(jnp.einsum('thkd,td->thk', u, xt).astype(f32), approximate=False)
        return jnp.einsum('thk,thkd->td', (gate * act).astype(vv.dtype), vv)

    y = lax.map(block, tokens)
    return y.reshape(bsz, seq, dm)


def setup_inputs(seed: int = 0) -> dict:
    key = jax.random.key(seed)
    ks = jax.random.split(key, 24)
    f32 = jnp.float32
    ns, na = N_SSD_LAYERS, N_ATTN_LAYERS

    def nrm(k, shape, scale):
        return jax.random.normal(k, shape, f32) * scale

    x = nrm(ks[0], (BATCH, SEQ, D_MODEL), 1.0)
    mix_norm_w = 1.0 + nrm(ks[1], (DEPTH, D_MODEL), 0.02)
    ffn_norm_w = 1.0 + nrm(ks[2], (DEPTH, D_MODEL), 0.02)
    ssd_in_w = nrm(ks[3], (ns, D_MODEL, SSD_IN_DIM), D_MODEL ** -0.5)
    ssd_conv_w = nrm(ks[4], (ns, SSD_CONV, SSD_CONV_DIM), SSD_CONV ** -0.5)
    ssd_conv_b = nrm(ks[5], (ns, SSD_CONV_DIM), 0.02)
    dt0 = jnp.exp(jax.random.uniform(ks[6], (ns, SSD_N_HEADS), f32)
                  * (math.log(DT_MAX) - math.log(DT_MIN)) + math.log(DT_MIN))
    ssd_dt_bias = dt0 + jnp.log(-jnp.expm1(-dt0))
    ssd_a_log = jnp.log(jax.random.uniform(ks[7], (ns, SSD_N_HEADS), f32, 1.0, 16.0))
    ssd_d = 1.0 + nrm(ks[8], (ns, SSD_N_HEADS), 0.02)
    ssd_norm_w = 1.0 + nrm(ks[9], (ns, SSD_D_INNER), 0.02)
    ssd_out_w = nrm(ks[10], (ns, SSD_D_INNER, D_MODEL), SSD_D_INNER ** -0.5)
    da_qkv_w = nrm(ks[11], (na, D_MODEL, 3 * D_MODEL), D_MODEL ** -0.5)
    da_q_norm_w = 1.0 + nrm(ks[12], (na, DA_HEAD_DIM), 0.02)
    da_k_norm_w = 1.0 + nrm(ks[13], (na, DA_HEAD_DIM), 0.02)
    da_lam_q1 = nrm(ks[14], (na, DA_HEAD_DIM), 0.1)
    da_lam_k1 = nrm(ks[15], (na, DA_HEAD_DIM), 0.1)
    da_lam_q2 = nrm(ks[16], (na, DA_HEAD_DIM), 0.1)
    da_lam_k2 = nrm(ks[17], (na, DA_HEAD_DIM), 0.1)
    da_subln_w = 1.0 + nrm(ks[18], (na, 2 * DA_HEAD_DIM), 0.02)
    da_out_w = nrm(ks[19], (na, D_MODEL, D_MODEL), D_MODEL ** -0.5)
    peer_query_w = nrm(ks[20], (DEPTH, D_MODEL, PEER_HEADS * PEER_D_KEY), D_MODEL ** -0.5)
    peer_sub_keys = nrm(ks[21], (DEPTH, PEER_HEADS, 2, PEER_N_KEYS, PEER_HALF), PEER_HALF ** -0.5)
    peer_u = nrm(ks[22], (DEPTH, PEER_N_EXPERTS, D_MODEL), D_MODEL ** -0.5)
    peer_v = nrm(ks[23], (DEPTH, PEER_N_EXPERTS, D_MODEL), PEER_HEADS ** -0.5)
    return {'x': x, 'mix_norm_w': mix_norm_w, 'ffn_norm_w': ffn_norm_w,
            'ssd_in_w': ssd_in_w, 'ssd_conv_w': ssd_conv_w, 'ssd_conv_b': ssd_conv_b,
            'ssd_dt_bias': ssd_dt_bias, 'ssd_a_log': ssd_a_log, 'ssd_d': ssd_d,
            'ssd_norm_w': ssd_norm_w, 'ssd_out_w': ssd_out_w,
            'da_qkv_w': da_qkv_w, 'da_q_norm_w': da_q_norm_w, 'da_k_norm_w': da_k_norm_w,
            'da_lam_q1': da_lam_q1, 'da_lam_k1': da_lam_k1, 'da_lam_q2': da_lam_q2,
            'da_lam_k2': da_lam_k2, 'da_subln_w': da_subln_w, 'da_out_w': da_out_w,
            'peer_query_w': peer_query_w, 'peer_sub_keys': peer_sub_keys,
            'peer_u': peer_u, 'peer_v': peer_v}


def reference(x, mix_norm_w, ffn_norm_w, ssd_in_w, ssd_conv_w, ssd_conv_b, ssd_dt_bias,
              ssd_a_log, ssd_d, ssd_norm_w, ssd_out_w, da_qkv_w, da_q_norm_w, da_k_norm_w,
              da_lam_q1, da_lam_k1, da_lam_q2, da_lam_k2, da_subln_w, da_out_w,
              peer_query_w, peer_sub_keys, peer_u, peer_v):
    for i in range(DEPTH):
        hn = rms_norm(x, mix_norm_w[i])
        j = i // N_MIXERS
        if i % N_MIXERS == 0:
            y = ssd_mixer(hn, ssd_in_w[j], ssd_conv_w[j], ssd_conv_b[j], ssd_dt_bias[j],
                          ssd_a_log[j], ssd_d[j], ssd_norm_w[j], ssd_out_w[j])
        else:
            lambda_init = 0.8 - 0.6 * math.exp(-0.3 * i)
            y = diff_attention(hn, da_qkv_w[j], da_q_norm_w[j], da_k_norm_w[j], da_lam_q1[j],
                               da_lam_k1[j], da_lam_q2[j], da_lam_k2[j], da_subln_w[j],
                               da_out_w[j], lambda_init)
        x = x + y
        hn = rms_norm(x, ffn_norm_w[i])
        x = x + peer_ffn(hn, peer_query_w[i], peer_sub_keys[i], peer_u[i], peer_v[i])
    return x
```

```python
import functools
import math

import jax
import jax.numpy as jnp
from jax import lax
from jax.experimental import pallas as pl
from jax.experimental.pallas import tpu as pltpu

F32 = jnp.float32
BF16 = jnp.bfloat16
I32 = jnp.int32
U32 = jnp.uint32

RMS_EPS = 1e-6
CHUNK = 64

SSD_HEAD_DIM = 64
SSD_N_HEADS = 32
SSD_N_GROUPS = 8
SSD_D_STATE = 128
SSD_D_INNER = 2048
SSD_CONV = 4
SSD_GROUP_W = SSD_D_INNER // SSD_N_GROUPS

DA_N_HEADS = 8
DA_HEAD_DIM = 64

PEER_N_KEYS = 128
PEER_HEADS = 8
PEER_TOPK = 16
PEER_PICKS = PEER_HEADS * PEER_TOPK
PEER_ROWS_PER_EXPERT = 4

VMEM_LIMIT = 56 * 1024 * 1024

NT_DIMS = (((1,), (1,)), ((), ()))


def _cparams(sem):
    return pltpu.CompilerParams(dimension_semantics=sem, vmem_limit_bytes=VMEM_LIMIT)


def _norm_matmul_kernel(x_ref, nw_ref, w_ref, o_ref, *rest, emit_norm):
    if emit_norm:
        hn_ref, xn_ref = rest
    else:
        (xn_ref,) = rest
    j = pl.program_id(1)

    @pl.when(j == 0)
    def _():
        x = x_ref[...]
        ms = jnp.mean(x * x, axis=-1, keepdims=True)
        xn = x * lax.rsqrt(ms + RMS_EPS) * nw_ref[...]
        xn_ref[...] = xn.astype(BF16)
        if emit_norm:
            hn_ref[...] = xn

    o_ref[...] = jnp.dot(xn_ref[...], w_ref[...], preferred_element_type=F32).astype(o_ref.dtype)


def norm_matmul(x, nw, w_bf16, out_dtype, *, tm=512, tn=512, emit_norm=False):
    t, d = x.shape
    n = w_bf16.shape[1]
    tn = min(tn, n)
    assert t % tm == 0 and n % tn == 0
    out_shape = [jax.ShapeDtypeStruct((t, n), out_dtype)]
    out_specs = [pl.BlockSpec((tm, tn), lambda i, j: (i, j))]
    if emit_norm:
        out_shape.append(jax.ShapeDtypeStruct((t, d), F32))
        out_specs.append(pl.BlockSpec((tm, d), lambda i, j: (i, 0)))
    res = pl.pallas_call(
        functools.partial(_norm_matmul_kernel, emit_norm=emit_norm),
        grid=(t // tm, n // tn),
        in_specs=[pl.BlockSpec((tm, d), lambda i, j: (i, 0)),
                  pl.BlockSpec((1, d), lambda i, j: (0, 0)),
                  pl.BlockSpec((d, tn), lambda i, j: (0, j))],
        out_specs=out_specs,
        out_shape=out_shape,
        scratch_shapes=[pltpu.VMEM((tm, d), BF16)],
        compiler_params=_cparams(("parallel", "arbitrary")),
        name="norm_matmul",
    )(x, nw.reshape(1, d), w_bf16)
    return res if emit_norm else res[0]


def _norm_matmul_qk_kernel(x_ref, nw_ref, w_ref, seg_ref, hw_ref, o_ref, xn_ref, *, n_norm_blocks):
    j = pl.program_id(1)

    @pl.when(j == 0)
    def _():
        x = x_ref[...]
        ms = jnp.mean(x * x, axis=-1, keepdims=True)
        xn_ref[...] = (x * lax.rsqrt(ms + RMS_EPS) * nw_ref[...]).astype(BF16)

    acc = jnp.dot(xn_ref[...], w_ref[...], preferred_element_type=F32)

    @pl.when(j < n_norm_blocks)
    def _():
        ss = jnp.dot((acc * acc).astype(BF16), seg_ref[...], preferred_element_type=F32)
        o_ref[...] = (acc * lax.rsqrt(ss * (1.0 / DA_HEAD_DIM) + RMS_EPS) * hw_ref[...]).astype(o_ref.dtype)

    @pl.when(j >= n_norm_blocks)
    def _():
        o_ref[...] = acc.astype(o_ref.dtype)


def norm_matmul_qk(x, nw, w_bf16, head_w, n_norm_cols, *, tm=512, tn=512):
    t, d = x.shape
    n = w_bf16.shape[1]
    assert t % tm == 0 and n % tn == 0 and n_norm_cols % tn == 0
    lane = jnp.arange(tn)
    seg = (lane[:, None] // DA_HEAD_DIM == lane[None, :] // DA_HEAD_DIM).astype(BF16)
    return pl.pallas_call(
        functools.partial(_norm_matmul_qk_kernel, n_norm_blocks=n_norm_cols // tn),
        grid=(t // tm, n // tn),
        in_specs=[pl.BlockSpec((tm, d), lambda i, j: (i, 0)),
                  pl.BlockSpec((1, d), lambda i, j: (0, 0)),
                  pl.BlockSpec((d, tn), lambda i, j: (0, j)),
                  pl.BlockSpec((tn, tn), lambda i, j: (0, 0)),
                  pl.BlockSpec((1, tn), lambda i, j: (0, j))],
        out_specs=pl.BlockSpec((tm, tn), lambda i, j: (i, j)),
        out_shape=jax.ShapeDtypeStruct((t, n), BF16),
        scratch_shapes=[pltpu.VMEM((tm, d), BF16)],
        compiler_params=_cparams(("parallel", "arbitrary")),
        name="norm_matmul_qk",
    )(x, nw.reshape(1, d), w_bf16, seg, head_w.reshape(1, n))


def _matmul_res_kernel(a_ref, w_ref, r_ref, o_ref):
    o_ref[...] = r_ref[...] + jnp.dot(a_ref[...], w_ref[...], preferred_element_type=F32)


def matmul_res(a_bf16, w_bf16, res, *, tm=512):
    t, k = a_bf16.shape
    n = w_bf16.shape[1]
    assert t % tm == 0
    return pl.pallas_call(
        _matmul_res_kernel,
        grid=(t // tm,),
        in_specs=[pl.BlockSpec((tm, k), lambda i: (i, 0)),
                  pl.BlockSpec((k, n), lambda i: (0, 0)),
                  pl.BlockSpec((tm, n), lambda i: (i, 0))],
        out_specs=pl.BlockSpec((tm, n), lambda i: (i, 0)),
        out_shape=jax.ShapeDtypeStruct((t, n), F32),
        compiler_params=_cparams(("parallel",)),
        name="matmul_res",
    )(a_bf16, w_bf16, res)


def _silu(v):
    return v * jax.nn.sigmoid(v)


def _ssd_kernel(z_ref, xs_ref, bc_ref, dt_ref, cwx_ref, cbx_ref, cwb_ref, cbb_ref, dtb_ref, alog_ref,
                expand_ref, dskip_ref, nw_ref, o_ref,
                state_ref, xtail_ref, btail_ref, xpad_ref, bpad_ref):
    c = pl.program_id(1)
    lc = CHUNK

    @pl.when(c == 0)
    def _():
        state_ref[...] = jnp.zeros_like(state_ref)
        xtail_ref[...] = jnp.zeros_like(xtail_ref)
        btail_ref[...] = jnp.zeros_like(btail_ref)

    def conv(x_ref, tail_ref, pad_ref, w_ref, b_ref):
        xin = x_ref[...].astype(F32)
        pad_ref[0:8, :] = tail_ref[...]
        pad_ref[8:8 + lc, :] = xin
        tail_ref[...] = xin[lc - 8:lc, :]
        acc = jnp.broadcast_to(b_ref[...], xin.shape)
        for k in range(SSD_CONV):
            off = 8 - (SSD_CONV - 1) + k
            acc = acc + w_ref[k:k + 1, :] * pad_ref[off:off + lc, :]
        return _silu(acc)

    xs = conv(xs_ref, xtail_ref, xpad_ref, cwx_ref, cbx_ref)
    bc = conv(bc_ref, btail_ref, bpad_ref, cwb_ref, cbb_ref)

    dt = jax.nn.softplus(dt_ref[...] + dtb_ref[...])
    adt = dt * (-jnp.exp(alog_ref[...]))
    r = lax.broadcasted_iota(I32, (lc, lc), 0)
    s = lax.broadcasted_iota(I32, (lc, lc), 1)
    tril = (r >= s).astype(F32)
    a_cs = jnp.dot(tril, adt, preferred_element_type=F32, precision=lax.Precision.HIGHEST)
    both = jnp.concatenate([dt, a_cs], axis=0)
    both_x = jnp.dot(both, expand_ref[...], preferred_element_type=F32,
                     precision=lax.Precision.HIGHEST)
    dt_x = both_x[0:lc, :]
    acs_x = both_x[lc:2 * lc, :]
    rr = lax.broadcasted_iota(I32, (lc, SSD_D_INNER), 0)
    ll = lax.broadcasted_iota(I32, (lc, SSD_D_INNER), 1)
    rowvec = jnp.sum(jnp.where((ll % lc) == rr, acs_x, 0.0), axis=0, keepdims=True)
    alast_x = acs_x[lc - 1:lc, :]

    lrow = lax.broadcasted_iota(I32, (lc, 128), 0)
    llane = lax.broadcasted_iota(I32, (lc, 128), 1)
    causal2 = (llane % lc) <= lrow
    brow = lax.broadcasted_iota(I32, (128, 128), 0)
    blane = lax.broadcasted_iota(I32, (128, 128), 1)
    blockdiag = (brow // lc) == (blane // lc)

    for g in range(SSD_N_GROUPS):
        lo = g * SSD_GROUP_W
        xs_g = xs[:, lo:lo + SSD_GROUP_W]
        b_g = bc[:, g * SSD_D_STATE:(g + 1) * SSD_D_STATE]
        c_g = bc[:, 1024 + g * SSD_D_STATE:1024 + (g + 1) * SSD_D_STATE]
        dt_g = dt_x[:, lo:lo + SSD_GROUP_W]
        acs_g = acs_x[:, lo:lo + SSD_GROUP_W]
        row_g = rowvec[:, lo:lo + SSD_GROUP_W]
        alast_g = alast_x[:, lo:lo + SSD_GROUP_W]

        xd_g = xs_g * dt_g
        cb = c_g.astype(BF16)
        bb = b_g.astype(BF16)
        st = state_ref[g]
        y_off = jnp.dot(cb, st.astype(BF16), preferred_element_type=F32) * jnp.exp(acs_g)
        b2 = jnp.concatenate([bb, bb], axis=0)
        cb2 = lax.dot_general(cb, b2, NT_DIMS, preferred_element_type=F32)
        yd = []
        for q in range(2):
            ql = q * 128
            seg = acs_g[:, ql:ql + 128] - row_g[:, ql:ql + 128]
            m = cb2 * jnp.exp(jnp.where(causal2, seg, -jnp.inf))
            xp = xd_g[:, ql:ql + 128].astype(BF16)
            x2 = jnp.concatenate([xp, xp], axis=0)
            x2 = jnp.where(blockdiag, x2, jnp.zeros_like(x2))
            yd.append(jnp.dot(m.astype(BF16), x2, preferred_element_type=F32))
        y_diag = jnp.concatenate(yd, axis=-1)

        xdw = (xd_g * jnp.exp(alast_g - acs_g)).astype(BF16)
        upd = jnp.dot(b_g.T.astype(BF16), xdw, preferred_element_type=F32)
        state_ref[g] = st * jnp.exp(alast_g) + upd

        y_g = y_diag + y_off + xs_g * dskip_ref[:, lo:lo + SSD_GROUP_W]
        y_g = y_g * _silu(z_ref[:, lo:lo + SSD_GROUP_W].astype(F32))
        ms = jnp.mean(y_g * y_g, axis=-1, keepdims=True)
        y_g = y_g * lax.rsqrt(ms + RMS_EPS) * nw_ref[:, lo:lo + SSD_GROUP_W]
        o_ref[:, lo:lo + SSD_GROUP_W] = y_g.astype(o_ref.dtype)


def ssd_core(zx, dt_raw, conv_w, conv_b, dt_bias, a_log, d_skip, norm_w, bsz, seq):
    t = zx.shape[0]
    nc = seq // CHUNK
    w = SSD_D_INNER
    pad = 128 - SSD_N_HEADS
    head = jnp.arange(128)
    expand = (head[:, None] == (jnp.arange(w)[None, :] // SSD_HEAD_DIM)).astype(F32)
    row = lambda v: v.reshape(1, -1).astype(F32)
    args = (zx, zx, zx, dt_raw,
            conv_w[:, :w], row(conv_b[:w]), conv_w[:, w:], row(conv_b[w:]),
            row(jnp.pad(dt_bias, (0, pad))), row(jnp.pad(a_log, (0, pad))),
            expand, row(jnp.repeat(d_skip, SSD_HEAD_DIM)), row(norm_w))
    const = lambda shape: pl.BlockSpec(shape, lambda b, c: (0, 0))
    return pl.pallas_call(
        _ssd_kernel,
        grid=(bsz, nc),
        in_specs=[pl.BlockSpec((CHUNK, w), lambda b, c: (b * nc + c, 0)),
                  pl.BlockSpec((CHUNK, w), lambda b, c: (b * nc + c, 1)),
                  pl.BlockSpec((CHUNK, w), lambda b, c: (b * nc + c, 2)),
                  pl.BlockSpec((CHUNK, 128), lambda b, c: (b * nc + c, 0)),
                  const((SSD_CONV, w)), const((1, w)), const((SSD_CONV, w)), const((1, w)),
                  const((1, 128)), const((1, 128)), const((128, w)), const((1, w)), const((1, w))],
        out_specs=pl.BlockSpec((CHUNK, w), lambda b, c: (b * nc + c, 0)),
        out_shape=jax.ShapeDtypeStruct((t, w), BF16),
        scratch_shapes=[pltpu.VMEM((SSD_N_GROUPS, SSD_D_STATE, SSD_GROUP_W), F32),
                        pltpu.VMEM((8, w), F32), pltpu.VMEM((8, w), F32),
                        pltpu.VMEM((8 + CHUNK, w), F32), pltpu.VMEM((8 + CHUNK, w), F32)],
        compiler_params=_cparams(("parallel", "arbitrary")),
        name="ssd_core",
    )(*args)


def _attn_kernel(q_ref, k_ref, v_ref, lamp_ref, sw_ref, o_ref,
                 m0_ref, l0_ref, a0_ref, m1_ref, l1_ref, a1_ref, *, tq, tk, lambda_init):
    i = pl.program_id(2)
    q = q_ref[...]
    lane = lax.broadcasted_iota(I32, q.shape, 1)
    zero = jnp.zeros_like(q)
    qs = (jnp.where(lane < DA_HEAD_DIM, q, zero), jnp.where(lane >= DA_HEAD_DIM, q, zero))
    stats = ((m0_ref, l0_ref, a0_ref), (m1_ref, l1_ref, a1_ref))
    for m_ref, l_ref, a_ref in stats:
        m_ref[...] = jnp.full_like(m_ref, -jnp.inf)
        l_ref[...] = jnp.zeros_like(l_ref)
        a_ref[...] = jnp.zeros_like(a_ref)

    def step(j, masked):
        start = pl.multiple_of(j * tk, tk)
        kb = k_ref[pl.ds(start, tk), :]
        vb = v_ref[pl.ds(start, tk), :]
        if masked:
            rq = lax.broadcasted_iota(I32, (tq, tk), 0) // CHUNK
            ck = lax.broadcasted_iota(I32, (tq, tk), 1) // CHUNK
            visible = ck <= rq
        for mi in range(2):
            m_ref, l_ref, a_ref = stats[mi]
            s = lax.dot_general(qs[mi], kb, NT_DIMS, preferred_element_type=F32)
            if masked:
                s = jnp.where(visible, s, -jnp.inf)
            m_prev = m_ref[...]
            m_next = jnp.maximum(m_prev, jnp.max(s, axis=1, keepdims=True))
            p = jnp.exp(s - m_next[:, 0:1])
            alpha = jnp.exp(m_prev - m_next)
            l_ref[...] = alpha * l_ref[...] + jnp.sum(p, axis=1, keepdims=True)
            a_ref[...] = alpha * a_ref[...] + jnp.dot(p.astype(BF16), vb, preferred_element_type=F32)
            m_ref[...] = m_next

    def body(j, carry):
        step(j, False)
        return carry

    lax.fori_loop(0, i, body, 0)
    step(i, True)

    lp = lamp_ref[...]
    lam = (jnp.exp(jnp.sum(lp[0:1] * lp[1:2], axis=-1, keepdims=True))
           - jnp.exp(jnp.sum(lp[2:3] * lp[3:4], axis=-1, keepdims=True)) + lambda_init)
    o = a0_ref[...] / l0_ref[...] - lam * (a1_ref[...] / l1_ref[...])
    ms = jnp.mean(o * o, axis=-1, keepdims=True)
    o = o * lax.rsqrt(ms + RMS_EPS) * sw_ref[...] * (1.0 - lambda_init)
    o_ref[...] = o.astype(o_ref.dtype)


def diff_attention_core(qkv, lam_params, subln_w, lambda_init, bsz, seq, *, tq=512):
    t = qkv.shape[0]
    tk = tq
    nq = seq // tq
    nh = DA_N_HEADS
    hw = 2 * DA_HEAD_DIM
    stat = lambda: pltpu.VMEM((tq, hw), F32)
    return pl.pallas_call(
        functools.partial(_attn_kernel, tq=tq, tk=tk, lambda_init=lambda_init),
        grid=(bsz, nh, nq),
        in_specs=[pl.BlockSpec((tq, hw), lambda b, h, i: (b * nq + i, h)),
                  pl.BlockSpec((seq, hw), lambda b, h, i: (b, nh + h)),
                  pl.BlockSpec((seq, hw), lambda b, h, i: (b, 2 * nh + h)),
                  pl.BlockSpec((4, DA_HEAD_DIM), lambda b, h, i: (0, 0)),
                  pl.BlockSpec((1, hw), lambda b, h, i: (0, 0))],
        out_specs=pl.BlockSpec((tq, hw), lambda b, h, i: (b * nq + i, h)),
        out_shape=jax.ShapeDtypeStruct((t, nh * hw), BF16),
        scratch_shapes=[stat() for _ in range(6)],
        compiler_params=_cparams(("parallel", "parallel", "arbitrary")),
        name="diff_attention",
    )(qkv, qkv, qkv, lam_params, subln_w.reshape(1, hw))


def _peer_topk_kernel(q_ref, keys_ref, idx_ref, gate_ref, ts_ref, ti_ref, bi_ref, bg_ref, *, tt):
    kk = PEER_TOPK
    nk = PEER_N_KEYS
    key_iota = lax.broadcasted_iota(I32, (nk, tt), 0)
    pos_iota = lax.broadcasted_iota(I32, (kk * kk, tt), 0)
    neg_inf = jnp.float32(-jnp.inf)
    for h in range(PEER_HEADS):
        for c in range(2):
            hc = 2 * h + c
            s = lax.dot_general(keys_ref[hc], q_ref[:, hc * 128:(hc + 1) * 128], NT_DIMS,
                                preferred_element_type=F32)
            for k in range(kk):
                m = jnp.max(s, axis=0, keepdims=True)
                first = jnp.min(jnp.where(s == m, key_iota, nk), axis=0, keepdims=True)
                s = jnp.where(key_iota == first, neg_inf, s)
                ts_ref[c, k:k + 1, :] = m
                ti_ref[c, k:k + 1, :] = first
        s1 = ts_ref[1]
        i1 = ti_ref[1]
        cand_s = jnp.concatenate([ts_ref[0, i:i + 1, :] + s1 for i in range(kk)], axis=0)
        cand_i = jnp.concatenate([ti_ref[0, i:i + 1, :] * nk + i1 for i in range(kk)], axis=0)
        best = []
        for k in range(kk):
            m = jnp.max(cand_s, axis=0, keepdims=True)
            first = jnp.min(jnp.where(cand_s == m, pos_iota, kk * kk), axis=0, keepdims=True)
            sel = pos_iota == first
            bi_ref[h * kk + k:h * kk + k + 1, :] = jnp.max(jnp.where(sel, cand_i, -1), axis=0, keepdims=True)
            cand_s = jnp.where(sel, neg_inf, cand_s)
            best.append(m)
        bs = jnp.concatenate(best, axis=0)
        e = jnp.exp(bs - best[0])
        bg_ref[h * kk:(h + 1) * kk, :] = e / jnp.sum(e, axis=0, keepdims=True)
    idx_ref[...] = bi_ref[...].T
    gate_ref[...] = bg_ref[...].T


def peer_topk(q_bf16, keys_bf16, *, tt=128):
    t = q_bf16.shape[0]
    assert t % tt == 0
    return pl.pallas_call(
        functools.partial(_peer_topk_kernel, tt=tt),
        grid=(t // tt,),
        in_specs=[pl.BlockSpec((tt, q_bf16.shape[1]), lambda i: (i, 0)),
                  pl.BlockSpec(keys_bf16.shape, lambda i: (0, 0, 0))],
        out_specs=[pl.BlockSpec((tt, PEER_PICKS), lambda i: (i, 0)),
                   pl.BlockSpec((tt, PEER_PICKS), lambda i: (i, 0))],
        out_shape=[jax.ShapeDtypeStruct((t, PEER_PICKS), I32),
                   jax.ShapeDtypeStruct((t, PEER_PICKS), F32)],
        scratch_shapes=[pltpu.VMEM((2, PEER_TOPK, tt), F32), pltpu.VMEM((2, PEER_TOPK, tt), I32),
                        pltpu.VMEM((PEER_PICKS, tt), I32), pltpu.VMEM((PEER_PICKS, tt), F32)],
        compiler_params=_cparams(("parallel",)),
        name="peer_topk",
    )(q_bf16, keys_bf16)


def pack_expert_table(tab):
    n, d = tab.shape
    tb = tab.astype(BF16).reshape(n, d // 256, 2, 128)
    lo = lax.bitcast_convert_type(tb[:, :, 0, :], jnp.uint16).astype(U32)
    hi = lax.bitcast_convert_type(tb[:, :, 1, :], jnp.uint16).astype(U32)
    return (lo | (hi << 16)).reshape(n * (d // 256), 128)


def _gather_token(idx_ref, tab_ref, buf_ref, tok):
    base = tok * PEER_PICKS
    r = PEER_ROWS_PER_EXPERT
    for k in range(PEER_PICKS):
        row = pl.multiple_of(idx_ref[base + k] * r, r)
        buf_ref[k * r:(k + 1) * r, :] = tab_ref[pl.ds(row, r), :]
    return pltpu.bitcast(buf_ref[...], BF16)


def _segment_mask(rows, cols):
    sub = lax.broadcasted_iota(I32, (rows, cols), 0)
    lane = lax.broadcasted_iota(I32, (rows, cols), 1)
    return (lane % rows) == sub


def _peer_u_kernel(idx_ref, hn_ref, gate_ref, tab_ref, fold_ref, w_ref, buf0_ref, buf1_ref, r_ref, *, tt):
    mask = _segment_mask(8, 8 * PEER_PICKS)

    def one(tok, buf_ref):
        wt = _gather_token(idx_ref, tab_ref, buf_ref, tok)
        xm = hn_ref[pl.ds(pl.multiple_of(tok * 8, 8), 8), :].astype(BF16)
        y = lax.dot_general(xm, wt, NT_DIMS, preferred_element_type=F32)
        r_ref[pl.ds(tok, 1), :] = jnp.sum(jnp.where(mask, y, 0.0), axis=0, keepdims=True)

    def body(it, carry):
        one(2 * it, buf0_ref)
        one(2 * it + 1, buf1_ref)
        return carry

    lax.fori_loop(0, tt // 2, body, 0)
    act = jnp.dot(r_ref[...], fold_ref[...], preferred_element_type=F32, precision=lax.Precision.HIGHEST)
    w_ref[...] = gate_ref[...] * (0.5 * act * (1.0 + lax.erf(act * (2.0 ** -0.5))))


def _peer_v_kernel(idx_ref, w_ref, x_ref, tab_ref, spread_ref, o_ref, buf0_ref, buf1_ref, wx_ref, *, tt):
    mask = _segment_mask(8, 8 * PEER_PICKS)
    wx_ref[...] = jnp.dot(w_ref[...].astype(BF16), spread_ref[...], preferred_element_type=F32)

    def one(tok, buf_ref):
        vt = _gather_token(idx_ref, tab_ref, buf_ref, tok)
        wrow = jnp.broadcast_to(wx_ref[pl.ds(tok, 1), :], mask.shape)
        wsel = jnp.where(mask, wrow, 0.0).astype(BF16)
        rows = pl.ds(pl.multiple_of(tok * 8, 8), 8)
        o_ref[rows, :] = x_ref[rows, :] + jnp.dot(wsel, vt, preferred_element_type=F32)

    def body(it, carry):
        one(2 * it, buf0_ref)
        one(2 * it + 1, buf1_ref)
        return carry

    lax.fori_loop(0, tt // 2, body, 0)


def _table_spec(shape):
    return pl.BlockSpec(shape, lambda i: (0, 0), pipeline_mode=pl.Buffered(1))


def peer_u(idx_flat, hn8, gate, utab, *, tt=64):
    t = gate.shape[0]
    assert t % tt == 0
    pick = jnp.arange(8 * PEER_PICKS) // 8
    fold = (pick[:, None] == jnp.arange(PEER_PICKS)[None, :]).astype(F32)
    gbuf = lambda: pltpu.VMEM((PEER_PICKS * PEER_ROWS_PER_EXPERT, 128), U32)
    return pl.pallas_call(
        functools.partial(_peer_u_kernel, tt=tt),
        grid=(t // tt,),
        in_specs=[pl.BlockSpec((tt * PEER_PICKS,), lambda i: (i,), memory_space=pltpu.SMEM),
                  pl.BlockSpec((tt * 8, 128), lambda i: (i, 0)),
                  pl.BlockSpec((tt, PEER_PICKS), lambda i: (i, 0)),
                  _table_spec(utab.shape),
                  pl.BlockSpec(fold.shape, lambda i: (0, 0))],
        out_specs=pl.BlockSpec((tt, PEER_PICKS), lambda i: (i, 0)),
        out_shape=jax.ShapeDtypeStruct((t, PEER_PICKS), F32),
        scratch_shapes=[gbuf(), gbuf(), pltpu.VMEM((tt, 8 * PEER_PICKS), F32)],
        compiler_params=_cparams(("arbitrary",)),
        name="peer_u",
    )(idx_flat, hn8, gate, utab, fold)


def peer_v(idx_flat, w, x8, vtab, *, tt=64):
    t = w.shape[0]
    assert t % tt == 0
    pick = jnp.arange(8 * PEER_PICKS) // 8
    spread = (jnp.arange(PEER_PICKS)[:, None] == pick[None, :]).astype(BF16)
    gbuf = lambda: pltpu.VMEM((PEER_PICKS * PEER_ROWS_PER_EXPERT, 128), U32)
    return pl.pallas_call(
        functools.partial(_peer_v_kernel, tt=tt),
        grid=(t // tt,),
        in_specs=[pl.BlockSpec((tt * PEER_PICKS,), lambda i: (i,), memory_space=pltpu.SMEM),
                  pl.BlockSpec((tt, PEER_PICKS), lambda i: (i, 0)),
                  pl.BlockSpec((tt * 8, 128), lambda i: (i, 0)),
                  _table_spec(vtab.shape),
                  pl.BlockSpec(spread.shape, lambda i: (0, 0))],
        out_specs=pl.BlockSpec((tt * 8, 128), lambda i: (i, 0)),
        out_shape=jax.ShapeDtypeStruct(x8.shape, F32),
        scratch_shapes=[gbuf(), gbuf(), pltpu.VMEM((tt, 8 * PEER_PICKS), F32)],
        compiler_params=_cparams(("arbitrary",)),
        name="peer_v",
    )(idx_flat, w, x8, vtab, spread)


def peer_block(x, norm_w, query_w, sub_keys, expert_u, expert_v):
    t, d = x.shape
    q, hn = norm_matmul(x, norm_w, query_w.astype(BF16), BF16, emit_norm=True)
    keys = sub_keys.reshape(PEER_HEADS * 2, PEER_N_KEYS, -1).astype(BF16)
    idx, gate = peer_topk(q, keys)
    idx_flat = idx.reshape(t * PEER_PICKS)
    w = peer_u(idx_flat, hn.reshape(t * 8, 128), gate, pack_expert_table(expert_u))
    y = peer_v(idx_flat, w, x.reshape(t * 8, 128), pack_expert_table(expert_v))
    return y.reshape(t, d)


def ssd_layer(x, norm_w, in_w, conv_w, conv_b, dt_bias, a_log, d_skip, gnorm_w, out_w, bsz, seq):
    n_zx = SSD_D_INNER + conv_w.shape[1]
    zx = norm_matmul(x, norm_w, in_w[:, :n_zx].astype(BF16), BF16)
    w_dt = jnp.pad(in_w[:, n_zx:], ((0, 0), (0, 128 - SSD_N_HEADS))).astype(BF16)
    dt_raw = norm_matmul(x, norm_w, w_dt, F32)
    y = ssd_core(zx, dt_raw, conv_w, conv_b, dt_bias, a_log, d_skip, gnorm_w, bsz, seq)
    return matmul_res(y, out_w.astype(BF16), x)


def attn_layer(x, norm_w, qkv_w, q_norm_w, k_norm_w, lam_q1, lam_k1, lam_q2, lam_k2, subln_w, out_w,
               lambda_init, bsz, seq):
    d = x.shape[1]
    reps = d // DA_HEAD_DIM
    head_w = jnp.concatenate([jnp.tile(q_norm_w, reps) * (DA_HEAD_DIM ** -0.5), jnp.tile(k_norm_w, reps),
                              jnp.ones((d,), F32)])
    qkv = norm_matmul_qk(x, norm_w, qkv_w.astype(BF16), head_w, 2 * d)
    lam_params = jnp.stack([lam_q1, lam_k1, lam_q2, lam_k2]).astype(F32)
    o = diff_attention_core(qkv, lam_params, subln_w, lambda_init, bsz, seq)
    return matmul_res(o, out_w.astype(BF16), x)


def kernel(x, mix_norm_w, ffn_norm_w, ssd_in_w, ssd_conv_w, ssd_conv_b, ssd_dt_bias, ssd_a_log, ssd_d,
           ssd_norm_w, ssd_out_w, da_qkv_w, da_q_norm_w, da_k_norm_w, da_lam_q1, da_lam_k1, da_lam_q2,
           da_lam_k2, da_subln_w, da_out_w, peer_query_w, peer_sub_keys, peer_u, peer_v):
    bsz, seq, d = x.shape
    depth = mix_norm_w.shape[0]
    h = x.reshape(bsz * seq, d)
    for i in range(depth):
        j = i // 2
        if i % 2 == 0:
            h = ssd_layer(h, mix_norm_w[i], ssd_in_w[j], ssd_conv_w[j], ssd_conv_b[j], ssd_dt_bias[j],
                          ssd_a_log[j], ssd_d[j], ssd_norm_w[j], ssd_out_w[j], bsz, seq)
        else:
            lambda_init = 0.8 - 0.6 * math.exp(-0.3 * i)
            h = attn_layer(h, mix_norm_w[i], da_qkv_w[j], da_q_norm_w[j], da_k_norm_w[j], da_lam_q1[j],
                           da_lam_k1[j], da_lam_q2[j], da_lam_k2[j], da_subln_w[j], da_out_w[j],
                           lambda_init, bsz, seq)
        h = peer_block(h, ffn_norm_w[i], peer_query_w[i], peer_sub_keys[i], peer_u[i], peer_v[i])
    return h.reshape(bsz, seq, d)
```

```python
import functools
import math

import jax
import jax.numpy as jnp
from jax import lax
from jax.experimental import pallas as pl
from jax.experimental.pallas import tpu as pltpu

F32 = jnp.float32
BF16 = jnp.bfloat16
I32 = jnp.int32
U32 = jnp.uint32

RMS_EPS = 1e-6
CHUNK = 64

SSD_HEAD_DIM = 64
SSD_N_HEADS = 32
SSD_N_GROUPS = 8
SSD_D_STATE = 128
SSD_D_INNER = 2048
SSD_CONV = 4
SSD_GROUP_W = SSD_D_INNER // SSD_N_GROUPS

DA_N_HEADS = 8
DA_HEAD_DIM = 64

PEER_N_KEYS = 128
PEER_HEADS = 8
PEER_TOPK = 16
PEER_PICKS = PEER_HEADS * PEER_TOPK
PEER_ROWS_PER_EXPERT = 4

VMEM_LIMIT = 56 * 1024 * 1024

NT_DIMS = (((1,), (1,)), ((), ()))


def _cparams(sem):
    return pltpu.CompilerParams(dimension_semantics=sem, vmem_limit_bytes=VMEM_LIMIT)


def _norm_matmul_kernel(x_ref, nw_ref, w_ref, o_ref, *rest, emit_norm):
    if emit_norm:
        hn_ref, xn_ref = rest
    else:
        (xn_ref,) = rest
    j = pl.program_id(1)

    @pl.when(j == 0)
    def _():
        x = x_ref[...]
        ms = jnp.mean(x * x, axis=-1, keepdims=True)
        xn = x * lax.rsqrt(ms + RMS_EPS) * nw_ref[...]
        xn_ref[...] = xn.astype(BF16)
        if emit_norm:
            hn_ref[...] = xn

    o_ref[...] = jnp.dot(xn_ref[...], w_ref[...], preferred_element_type=F32).astype(o_ref.dtype)


def norm_matmul(x, nw, w_bf16, out_dtype, *, tm=512, tn=512, emit_norm=False):
    t, d = x.shape
    n = w_bf16.shape[1]
    tn = min(tn, n)
    assert t % tm == 0 and n % tn == 0
    out_shape = [jax.ShapeDtypeStruct((t, n), out_dtype)]
    out_specs = [pl.BlockSpec((tm, tn), lambda i, j: (i, j))]
    if emit_norm:
        out_shape.append(jax.ShapeDtypeStruct((t, d), F32))
        out_specs.append(pl.BlockSpec((tm, d), lambda i, j: (i, 0)))
    res = pl.pallas_call(
        functools.partial(_norm_matmul_kernel, emit_norm=emit_norm),
        grid=(t // tm, n // tn),
        in_specs=[pl.BlockSpec((tm, d), lambda i, j: (i, 0)),
                  pl.BlockSpec((1, d), lambda i, j: (0, 0)),
                  pl.BlockSpec((d, tn), lambda i, j: (0, j))],
        out_specs=out_specs,
        out_shape=out_shape,
        scratch_shapes=[pltpu.VMEM((tm, d), BF16)],
        compiler_params=_cparams(("parallel", "arbitrary")),
        name="norm_matmul",
    )(x, nw.reshape(1, d), w_bf16)
    return res if emit_norm else res[0]


def _norm_matmul_qk_kernel(x_ref, nw_ref, w_ref, seg_ref, hw_ref, o_ref, xn_ref, *, n_norm_blocks):
    j = pl.program_id(1)

    @pl.when(j == 0)
    def _():
        x = x_ref[...]
        ms = jnp.mean(x * x, axis=-1, keepdims=True)
        xn_ref[...] = (x * lax.rsqrt(ms + RMS_EPS) * nw_ref[...]).astype(BF16)

    acc = jnp.dot(xn_ref[...], w_ref[...], preferred_element_type=F32)

    @pl.when(j < n_norm_blocks)
    def _():
        ss = jnp.dot((acc * acc).astype(BF16), seg_ref[...], preferred_element_type=F32)
        o_ref[...] = (acc * lax.rsqrt(ss * (1.0 / DA_HEAD_DIM) + RMS_EPS) * hw_ref[...]).astype(o_ref.dtype)

    @pl.when(j >= n_norm_blocks)
    def _():
        o_ref[...] = acc.astype(o_ref.dtype)


def norm_matmul_qk(x, nw, w_bf16, head_w, n_norm_cols, *, tm=512, tn=512):
    t, d = x.shape
    n = w_bf16.shape[1]
    assert t % tm == 0 and n % tn == 0 and n_norm_cols % tn == 0
    lane = jnp.arange(tn)
    seg = (lane[:, None] // DA_HEAD_DIM == lane[None, :] // DA_HEAD_DIM).astype(BF16)
    return pl.pallas_call(
        functools.partial(_norm_matmul_qk_kernel, n_norm_blocks=n_norm_cols // tn),
        grid=(t // tm, n // tn),
        in_specs=[pl.BlockSpec((tm, d), lambda i, j: (i, 0)),
                  pl.BlockSpec((1, d), lambda i, j: (0, 0)),
                  pl.BlockSpec((d, tn), lambda i, j: (0, j)),
                  pl.BlockSpec((tn, tn), lambda i, j: (0, 0)),
                  pl.BlockSpec((1, tn), lambda i, j: (0, j))],
        out_specs=pl.BlockSpec((tm, tn), lambda i, j: (i, j)),
        out_shape=jax.ShapeDtypeStruct((t, n), BF16),
        scratch_shapes=[pltpu.VMEM((tm, d), BF16)],
        compiler_params=_cparams(("parallel", "arbitrary")),
        name="norm_matmul_qk",
    )(x, nw.reshape(1, d), w_bf16, seg, head_w.reshape(1, n))


def _matmul_res_kernel(a_ref, w_ref, r_ref, o_ref):
    o_ref[...] = r_ref[...] + jnp.dot(a_ref[...], w_ref[...], preferred_element_type=F32)


def matmul_res(a_bf16, w_bf16, res, *, tm=512):
    t, k = a_bf16.shape
    n = w_bf16.shape[1]
    assert t % tm == 0
    return pl.pallas_call(
        _matmul_res_kernel,
        grid=(t // tm,),
        in_specs=[pl.BlockSpec((tm, k), lambda i: (i, 0)),
                  pl.BlockSpec((k, n), lambda i: (0, 0)),
                  pl.BlockSpec((tm, n), lambda i: (i, 0))],
        out_specs=pl.BlockSpec((tm, n), lambda i: (i, 0)),
        out_shape=jax.ShapeDtypeStruct((t, n), F32),
        compiler_params=_cparams(("parallel",)),
        name="matmul_res",
    )(a_bf16, w_bf16, res)


def _silu(v):
    return v * jax.nn.sigmoid(v)


def _ssd_kernel(z_ref, xs_ref, bc_ref, dt_ref, cwx_ref, cbx_ref, cwb_ref, cbb_ref, dtb_ref, alog_ref,
                expand_ref, dskip_ref, nw_ref, o_ref,
                state_ref, xtail_ref, btail_ref, xpad_ref, bpad_ref):
    c = pl.program_id(1)
    lc = CHUNK

    @pl.when(c == 0)
    def _():
        state_ref[...] = jnp.zeros_like(state_ref)
        xtail_ref[...] = jnp.zeros_like(xtail_ref)
        btail_ref[...] = jnp.zeros_like(btail_ref)

    def conv(x_ref, tail_ref, pad_ref, w_ref, b_ref):
        xin = x_ref[...].astype(F32)
        pad_ref[0:8, :] = tail_ref[...]
        pad_ref[8:8 + lc, :] = xin
        tail_ref[...] = xin[lc - 8:lc, :]
        acc = jnp.broadcast_to(b_ref[...], xin.shape)
        for k in range(SSD_CONV):
            off = 8 - (SSD_CONV - 1) + k
            acc = acc + w_ref[k:k + 1, :] * pad_ref[off:off + lc, :]
        return _silu(acc)

    xs = conv(xs_ref, xtail_ref, xpad_ref, cwx_ref, cbx_ref)
    bc = conv(bc_ref, btail_ref, bpad_ref, cwb_ref, cbb_ref)

    dt = jax.nn.softplus(dt_ref[...] + dtb_ref[...])
    adt = dt * (-jnp.exp(alog_ref[...]))
    r = lax.broadcasted_iota(I32, (lc, lc), 0)
    s = lax.broadcasted_iota(I32, (lc, lc), 1)
    tril = (r >= s).astype(F32)
    a_cs = jnp.dot(tril, adt, preferred_element_type=F32, precision=lax.Precision.HIGHEST)
    both = jnp.concatenate([dt, a_cs], axis=0)
    both_x = jnp.dot(both, expand_ref[...], preferred_element_type=F32,
                     precision=lax.Precision.HIGHEST)
    dt_x = both_x[0:lc, :]
    acs_x = both_x[lc:2 * lc, :]
    rr = lax.broadcasted_iota(I32, (lc, SSD_D_INNER), 0)
    ll = lax.broadcasted_iota(I32, (lc, SSD_D_INNER), 1)
    rowvec = jnp.sum(jnp.where((ll % lc) == rr, acs_x, 0.0), axis=0, keepdims=True)
    alast_x = acs_x[lc - 1:lc, :]

    lrow = lax.broadcasted_iota(I32, (lc, 128), 0)
    llane = lax.broadcasted_iota(I32, (lc, 128), 1)
    causal2 = (llane % lc) <= lrow
    brow = lax.broadcasted_iota(I32, (128, 128), 0)
    blane = lax.broadcasted_iota(I32, (128, 128), 1)
    blockdiag = (brow // lc) == (blane // lc)

    for g in range(SSD_N_GROUPS):
        lo = g * SSD_GROUP_W
        xs_g = xs[:, lo:lo + SSD_GROUP_W]
        b_g = bc[:, g * SSD_D_STATE:(g + 1) * SSD_D_STATE]
        c_g = bc[:, 1024 + g * SSD_D_STATE:1024 + (g + 1) * SSD_D_STATE]
        dt_g = dt_x[:, lo:lo + SSD_GROUP_W]
        acs_g = acs_x[:, lo:lo + SSD_GROUP_W]
        row_g = rowvec[:, lo:lo + SSD_GROUP_W]
        alast_g = alast_x[:, lo:lo + SSD_GROUP_W]

        xd_g = xs_g * dt_g
        cb = c_g.astype(BF16)
        bb = b_g.astype(BF16)
        st = state_ref[g]
        y_off = jnp.dot(cb, st.astype(BF16), preferred_element_type=F32) * jnp.exp(acs_g)
        b2 = jnp.concatenate([bb, bb], axis=0)
        cb2 = lax.dot_general(cb, b2, NT_DIMS, preferred_element_type=F32)
        yd = []
        for q in range(2):
            ql = q * 128
            seg = acs_g[:, ql:ql + 128] - row_g[:, ql:ql + 128]
            m = cb2 * jnp.exp(jnp.where(causal2, seg, -jnp.inf))
            xp = xd_g[:, ql:ql + 128].astype(BF16)
            x2 = jnp.concatenate([xp, xp], axis=0)
            x2 = jnp.where(blockdiag, x2, jnp.zeros_like(x2))
            yd.append(jnp.dot(m.astype(BF16), x2, preferred_element_type=F32))
        y_diag = jnp.concatenate(yd, axis=-1)

        xdw = (xd_g * jnp.exp(alast_g - acs_g)).astype(BF16)
        upd = jnp.dot(b_g.T.astype(BF16), xdw, preferred_element_type=F32)
        state_ref[g] = st * jnp.exp(alast_g) + upd

        y_g = y_diag + y_off + xs_g * dskip_ref[:, lo:lo + SSD_GROUP_W]
        y_g = y_g * _silu(z_ref[:, lo:lo + SSD_GROUP_W].astype(F32))
        ms = jnp.mean(y_g * y_g, axis=-1, keepdims=True)
        y_g = y_g * lax.rsqrt(ms + RMS_EPS) * nw_ref[:, lo:lo + SSD_GROUP_W]
        o_ref[:, lo:lo + SSD_GROUP_W] = y_g.astype(o_ref.dtype)


def ssd_core(zx, dt_raw, conv_w, conv_b, dt_bias, a_log, d_skip, norm_w, bsz, seq):
    t = zx.shape[0]
    nc = seq // CHUNK
    w = SSD_D_INNER
    pad = 128 - SSD_N_HEADS
    head = jnp.arange(128)
    expand = (head[:, None] == (jnp.arange(w)[None, :] // SSD_HEAD_DIM)).astype(F32)
    row = lambda v: v.reshape(1, -1).astype(F32)
    args = (zx, zx, zx, dt_raw,
            conv_w[:, :w], row(conv_b[:w]), conv_w[:, w:], row(conv_b[w:]),
            row(jnp.pad(dt_bias, (0, pad))), row(jnp.pad(a_log, (0, pad))),
            expand, row(jnp.repeat(d_skip, SSD_HEAD_DIM)), row(norm_w))
    const = lambda shape: pl.BlockSpec(shape, lambda b, c: (0, 0))
    return pl.pallas_call(
        _ssd_kernel,
        grid=(bsz, nc),
        in_specs=[pl.BlockSpec((CHUNK, w), lambda b, c: (b * nc + c, 0)),
                  pl.BlockSpec((CHUNK, w), lambda b, c: (b * nc + c, 1)),
                  pl.BlockSpec((CHUNK, w), lambda b, c: (b * nc + c, 2)),
                  pl.BlockSpec((CHUNK, 128), lambda b, c: (b * nc + c, 0)),
                  const((SSD_CONV, w)), const((1, w)), const((SSD_CONV, w)), const((1, w)),
                  const((1, 128)), const((1, 128)), const((128, w)), const((1, w)), const((1, w))],
        out_specs=pl.BlockSpec((CHUNK, w), lambda b, c: (b * nc + c, 0)),
        out_shape=jax.ShapeDtypeStruct((t, w), BF16),
        scratch_shapes=[pltpu.VMEM((SSD_N_GROUPS, SSD_D_STATE, SSD_GROUP_W), F32),
                        pltpu.VMEM((8, w), F32), pltpu.VMEM((8, w), F32),
                        pltpu.VMEM((8 + CHUNK, w), F32), pltpu.VMEM((8 + CHUNK, w), F32)],
        compiler_params=_cparams(("parallel", "arbitrary")),
        name="ssd_core",
    )(*args)


DA_VT_ROWS = 2 * DA_HEAD_DIM + 16


def _attn_kernel(q_ref, k_ref, vt_ref, lamp_ref, sw_ref, o_ref, m_ref, a_ref, s0_ref, s1_ref, b0_ref, b1_ref,
                 *, tq, tk, lambda_init):
    i = pl.program_id(2)
    hw = 2 * DA_HEAD_DIM
    q = q_ref[...]
    lane = lax.broadcasted_iota(I32, q.shape, 1)
    zero = jnp.zeros_like(q)
    qs = (jnp.where(lane < DA_HEAD_DIM, q, zero), jnp.where(lane >= DA_HEAD_DIM, q, zero))
    m_ref[...] = jnp.full_like(m_ref, -jnp.inf)
    a_ref[...] = jnp.zeros_like(a_ref)
    slots = ((s0_ref, b0_ref), (s1_ref, b1_ref))

    def qk(j, slot, masked):
        st_ref, mb_ref = slots[slot]
        kb = k_ref[pl.ds(pl.multiple_of(j * tk, tk), tk), :]
        if masked:
            kc = lax.broadcasted_iota(I32, (tk, tq), 0) // CHUNK
            qc = lax.broadcasted_iota(I32, (tk, tq), 1) // CHUNK
            visible = kc <= qc
        for mi in range(2):
            st = lax.dot_general(kb, qs[mi], NT_DIMS, preferred_element_type=F32)
            if masked:
                st = jnp.where(visible, st, -jnp.inf)
            st_ref[mi] = st
            mb_ref[mi] = jnp.max(st, axis=0, keepdims=True)

    def pv(j, slot):
        st_ref, mb_ref = slots[slot]
        vt = vt_ref[j]
        for mi in range(2):
            m_prev = m_ref[mi]
            m_next = jnp.maximum(m_prev, mb_ref[mi])
            p = jnp.exp((st_ref[mi] - m_next).astype(BF16))
            alpha = jnp.exp(m_prev - m_next)
            a_ref[mi] = alpha * a_ref[mi] + jnp.dot(vt, p, preferred_element_type=F32)
            m_ref[mi] = m_next

    @pl.when(i == 0)
    def _():
        qk(0, 0, True)
        pv(0, 0)

    @pl.when(i > 0)
    def _():
        qk(0, 0, False)
        n_pairs = lax.shift_right_logical(i - 1, 1)

        def body(jj, carry):
            j = 2 * jj
            qk(j + 1, 1, False)
            pv(j, 0)
            qk(j + 2, 0, False)
            pv(j + 1, 1)
            return carry

        lax.fori_loop(0, n_pairs, body, 0)

        @pl.when(i % 2 == 1)
        def _():
            qk(i, 1, True)
            pv(i - 1, 0)
            pv(i, 1)

        @pl.when(i % 2 == 0)
        def _():
            qk(i - 1, 1, False)
            pv(i - 2, 0)
            qk(i, 0, True)
            pv(i - 1, 1)
            pv(i, 0)

    lp = lamp_ref[...]
    lam = (jnp.exp(jnp.sum(lp[0:1] * lp[1:2], axis=-1, keepdims=True))
           - jnp.exp(jnp.sum(lp[2:3] * lp[3:4], axis=-1, keepdims=True)) + lambda_init)
    a0 = a_ref[0]
    a1 = a_ref[1]
    ot = a0[0:hw] / a0[hw:hw + 1] - lam * (a1[0:hw] / a1[hw:hw + 1])
    ms = jnp.mean(ot * ot, axis=0, keepdims=True)
    ot = ot * lax.rsqrt(ms + RMS_EPS) * (sw_ref[...] * (1.0 - lambda_init))
    o_ref[...] = ot.T.astype(o_ref.dtype)


def diff_attention_core(qkv, lam_params, subln_w, lambda_init, bsz, seq, *, tq=512):
    t = qkv.shape[0]
    tk = tq
    nq = seq // tq
    nh = DA_N_HEADS
    hw = 2 * DA_HEAD_DIM
    v = qkv[:, 2 * nh * hw:].reshape(bsz, nq, tk, nh, hw)
    vt = jnp.transpose(v, (0, 3, 1, 4, 2))
    vt = jnp.concatenate([vt, jnp.ones((bsz, nh, nq, DA_VT_ROWS - hw, tk), BF16)], axis=3)
    vt = vt.reshape(bsz * nh * nq, DA_VT_ROWS, tk)
    return pl.pallas_call(
        functools.partial(_attn_kernel, tq=tq, tk=tk, lambda_init=lambda_init),
        grid=(bsz, nh, nq),
        in_specs=[pl.BlockSpec((tq, hw), lambda b, h, i: (b * nq + i, h)),
                  pl.BlockSpec((seq, hw), lambda b, h, i: (b, nh + h)),
                  pl.BlockSpec((nq, DA_VT_ROWS, tk), lambda b, h, i: (b * nh + h, 0, 0)),
                  pl.BlockSpec((4, DA_HEAD_DIM), lambda b, h, i: (0, 0)),
                  pl.BlockSpec((hw, 1), lambda b, h, i: (0, 0))],
        out_specs=pl.BlockSpec((tq, hw), lambda b, h, i: (b * nq + i, h)),
        out_shape=jax.ShapeDtypeStruct((t, nh * hw), BF16),
        scratch_shapes=[pltpu.VMEM((2, 1, tq), F32), pltpu.VMEM((2, DA_VT_ROWS, tq), F32),
                        pltpu.VMEM((2, tk, tq), F32), pltpu.VMEM((2, tk, tq), F32),
                        pltpu.VMEM((2, 1, tq), F32), pltpu.VMEM((2, 1, tq), F32)],
        compiler_params=_cparams(("parallel", "parallel", "arbitrary")),
        name="diff_attention",
    )(qkv, qkv, vt, lam_params, subln_w.reshape(hw, 1))


def _candidate_blocks():
    kk = PEER_TOPK
    blocks = [(0, 1, 0, 8), (0, 1, 8, 8)]
    blocks += [(i, 1, 0, min(8, kk // (i + 1))) for i in range(1, 8)]
    blocks.append((8, 8, 0, 1))
    return blocks


def _peer_topk_kernel(q_ref, keys_ref, idx_ref, gate_ref, ts_ref, ti_ref, bs_ref, bp_ref, bi_ref, bg_ref, *, tt):
    kk = PEER_TOPK
    nk = PEER_N_KEYS
    big = jnp.float32(1e9)
    neg_inf = jnp.float32(-jnp.inf)
    key_iota = lax.broadcasted_iota(I32, (nk, tt), 0).astype(F32)
    r8 = lax.broadcasted_iota(I32, (8, tt), 0).astype(F32)
    blocks = _candidate_blocks()
    pos = jnp.concatenate(
        [jnp.where(r8 < max(n_i, n_j), (i0 * kk + j0) + r8 * (kk if n_i > 1 else 1), big)
         for i0, n_i, j0, n_j in blocks], axis=0)
    for h in range(PEER_HEADS):
        for c in range(2):
            hc = 2 * h + c
            s = lax.dot_general(keys_ref[hc], q_ref[:, hc * 128:(hc + 1) * 128], NT_DIMS,
                                preferred_element_type=F32)
            for k in range(kk):
                m = jnp.max(s, axis=0, keepdims=True)
                first = jnp.min(jnp.where(s == m, key_iota, big), axis=0, keepdims=True)
                s = jnp.where(key_iota == first, neg_inf, s)
                ts_ref[c, k:k + 1, :] = m
                ti_ref[c, k:k + 1, :] = first
        cand = []
        for i0, n_i, j0, n_j in blocks:
            a = ts_ref[0, i0:i0 + n_i, :]
            b = ts_ref[1, j0:j0 + 8, :] if n_j > 1 else ts_ref[1, j0:j0 + 1, :]
            cand.append(jnp.where(r8 < max(n_i, n_j), a + b, neg_inf))
        cand = jnp.concatenate(cand, axis=0)
        for k in range(kk):
            m = jnp.max(cand, axis=0, keepdims=True)
            first = jnp.min(jnp.where(cand == m, pos, big), axis=0, keepdims=True)
            cand = jnp.where(pos == first, neg_inf, cand)
            bs_ref[k:k + 1, :] = m
            bp_ref[k:k + 1, :] = first
        p = bp_ref[...]
        pi = jnp.floor(p * (1.0 / kk))
        pj = p - pi * kk
        e0 = jnp.zeros_like(p)
        e1 = jnp.zeros_like(p)
        for r in range(kk):
            e0 = e0 + jnp.where(pi == r, ti_ref[0, r:r + 1, :], 0.0)
            e1 = e1 + jnp.where(pj == r, ti_ref[1, r:r + 1, :], 0.0)
        rows = (e0 * nk + e1) * PEER_ROWS_PER_EXPERT
        bi_ref[h * kk:(h + 1) * kk, :] = rows.astype(I32)
        bs = bs_ref[...]
        e = jnp.exp(bs - bs[0:1])
        bg_ref[h * kk:(h + 1) * kk, :] = e / jnp.sum(e, axis=0, keepdims=True)
    idx_ref[...] = bi_ref[...].T
    gate_ref[...] = bg_ref[...].T


def peer_topk(q_bf16, keys_bf16, *, tt=128):
    t = q_bf16.shape[0]
    assert t % tt == 0
    return pl.pallas_call(
        functools.partial(_peer_topk_kernel, tt=tt),
        grid=(t // tt,),
        in_specs=[pl.BlockSpec((tt, q_bf16.shape[1]), lambda i: (i, 0)),
                  pl.BlockSpec(keys_bf16.shape, lambda i: (0, 0, 0))],
        out_specs=[pl.BlockSpec((tt, PEER_PICKS), lambda i: (i, 0)),
                   pl.BlockSpec((tt, PEER_PICKS), lambda i: (i, 0))],
        out_shape=[jax.ShapeDtypeStruct((t, PEER_PICKS), I32),
                   jax.ShapeDtypeStruct((t, PEER_PICKS), F32)],
        scratch_shapes=[pltpu.VMEM((2, PEER_TOPK, tt), F32), pltpu.VMEM((2, PEER_TOPK, tt), F32),
                        pltpu.VMEM((PEER_TOPK, tt), F32), pltpu.VMEM((PEER_TOPK, tt), F32),
                        pltpu.VMEM((PEER_PICKS, tt), I32), pltpu.VMEM((PEER_PICKS, tt), F32)],
        compiler_params=_cparams(("parallel",)),
        name="peer_topk",
    )(q_bf16, keys_bf16)


def pack_expert_table(tab):
    n, d = tab.shape
    tb = tab.astype(BF16).reshape(n, d // 256, 2, 128)
    lo = lax.bitcast_convert_type(tb[:, :, 0, :], jnp.uint16).astype(U32)
    hi = lax.bitcast_convert_type(tb[:, :, 1, :], jnp.uint16).astype(U32)
    return (lo | (hi << 16)).reshape(n * (d // 256), 128)


def _gather_token(idx_ref, tab_ref, buf_ref, tok):
    base = tok * PEER_PICKS
    r = PEER_ROWS_PER_EXPERT
    for k in range(PEER_PICKS):
        row = pl.multiple_of(idx_ref[base + k], r)
        buf_ref[k * r:(k + 1) * r, :] = tab_ref[pl.ds(row, r), :]


def _gathered(buf_ref):
    return pltpu.bitcast(buf_ref[...], BF16)


def _pipelined_tokens(tt, gather, compute, bufs):
    a0, a1, b0, b1 = bufs
    assert tt % 4 == 0
    gather(0, b0)
    gather(1, b1)

    def quad(t, last):
        compute(t, b0)
        compute(t + 1, b1)
        gather(t + 2, a0)
        gather(t + 3, a1)
        compute(t + 2, a0)
        compute(t + 3, a1)
        if not last:
            gather(t + 4, b0)
            gather(t + 5, b1)

    def body(it, carry):
        quad(4 * it, False)
        return carry

    lax.fori_loop(0, tt // 4 - 1, body, 0)
    quad(tt - 4, True)


def _segment_mask(rows, cols):
    sub = lax.broadcasted_iota(I32, (rows, cols), 0)
    lane = lax.broadcasted_iota(I32, (rows, cols), 1)
    return (lane % rows) == sub


def _peer_u_kernel(idx_ref, hn_ref, gate_ref, tab_ref, fold_ref, w_ref, a0_ref, a1_ref, b0_ref, b1_ref, r_ref,
                   *, tt):
    mask = _segment_mask(8, 8 * PEER_PICKS)

    def gather(tok, buf_ref):
        _gather_token(idx_ref, tab_ref, buf_ref, tok)

    def compute(tok, buf_ref):
        xm = hn_ref[pl.ds(pl.multiple_of(tok * 8, 8), 8), :].astype(BF16)
        y = lax.dot_general(xm, _gathered(buf_ref), NT_DIMS, preferred_element_type=F32)
        r_ref[pl.ds(tok, 1), :] = jnp.sum(jnp.where(mask, y, 0.0), axis=0, keepdims=True)

    _pipelined_tokens(tt, gather, compute, (a0_ref, a1_ref, b0_ref, b1_ref))
    act = jnp.dot(r_ref[...], fold_ref[...], preferred_element_type=F32, precision=lax.Precision.HIGHEST)
    w_ref[...] = gate_ref[...] * (0.5 * act * (1.0 + lax.erf(act * (2.0 ** -0.5))))


def _peer_v_kernel(idx_ref, w_ref, x_ref, tab_ref, spread_ref, o_ref, a0_ref, a1_ref, b0_ref, b1_ref, wx_ref,
                   *, tt):
    mask = _segment_mask(8, 8 * PEER_PICKS)
    wx_ref[...] = jnp.dot(w_ref[...].astype(BF16), spread_ref[...], preferred_element_type=F32)

    def gather(tok, buf_ref):
        _gather_token(idx_ref, tab_ref, buf_ref, tok)

    def compute(tok, buf_ref):
        wrow = jnp.broadcast_to(wx_ref[pl.ds(tok, 1), :], mask.shape)
        wsel = jnp.where(mask, wrow, 0.0).astype(BF16)
        rows = pl.ds(pl.multiple_of(tok * 8, 8), 8)
        o_ref[rows, :] = x_ref[rows, :] + jnp.dot(wsel, _gathered(buf_ref), preferred_element_type=F32)

    _pipelined_tokens(tt, gather, compute, (a0_ref, a1_ref, b0_ref, b1_ref))


def _table_spec(shape):
    return pl.BlockSpec(shape, lambda i: (0, 0), pipeline_mode=pl.Buffered(1))


def peer_u(idx_flat, hn8, gate, utab, *, tt=64):
    t = gate.shape[0]
    assert t % tt == 0
    pick = jnp.arange(8 * PEER_PICKS) // 8
    fold = (pick[:, None] == jnp.arange(PEER_PICKS)[None, :]).astype(F32)
    gbuf = lambda: pltpu.VMEM((PEER_PICKS * PEER_ROWS_PER_EXPERT, 128), U32)
    return pl.pallas_call(
        functools.partial(_peer_u_kernel, tt=tt),
        grid=(t // tt,),
        in_specs=[pl.BlockSpec((tt * PEER_PICKS,), lambda i: (i,), memory_space=pltpu.SMEM),
                  pl.BlockSpec((tt * 8, 128), lambda i: (i, 0)),
                  pl.BlockSpec((tt, PEER_PICKS), lambda i: (i, 0)),
                  _table_spec(utab.shape),
                  pl.BlockSpec(fold.shape, lambda i: (0, 0))],
        out_specs=pl.BlockSpec((tt, PEER_PICKS), lambda i: (i, 0)),
        out_shape=jax.ShapeDtypeStruct((t, PEER_PICKS), F32),
        scratch_shapes=[gbuf(), gbuf(), gbuf(), gbuf(), pltpu.VMEM((tt, 8 * PEER_PICKS), F32)],
        compiler_params=_cparams(("arbitrary",)),
        name="peer_u",
    )(idx_flat, hn8, gate, utab, fold)


def peer_v(idx_flat, w, x8, vtab, *, tt=64):
    t = w.shape[0]
    assert t % tt == 0
    pick = jnp.arange(8 * PEER_PICKS) // 8
    spread = (jnp.arange(PEER_PICKS)[:, None] == pick[None, :]).astype(BF16)
    gbuf = lambda: pltpu.VMEM((PEER_PICKS * PEER_ROWS_PER_EXPERT, 128), U32)
    return pl.pallas_call(
        functools.partial(_peer_v_kernel, tt=tt),
        grid=(t // tt,),
        in_specs=[pl.BlockSpec((tt * PEER_PICKS,), lambda i: (i,), memory_space=pltpu.SMEM),
                  pl.BlockSpec((tt, PEER_PICKS), lambda i: (i, 0)),
                  pl.BlockSpec((tt * 8, 128), lambda i: (i, 0)),
                  _table_spec(vtab.shape),
                  pl.BlockSpec(spread.shape, lambda i: (0, 0))],
        out_specs=pl.BlockSpec((tt * 8, 128), lambda i: (i, 0)),
        out_shape=jax.ShapeDtypeStruct(x8.shape, F32),
        scratch_shapes=[gbuf(), gbuf(), gbuf(), gbuf(), pltpu.VMEM((tt, 8 * PEER_PICKS), F32)],
        compiler_params=_cparams(("arbitrary",)),
        name="peer_v",
    )(idx_flat, w, x8, vtab, spread)


def peer_block(x, norm_w, query_w, sub_keys, expert_u, expert_v):
    t, d = x.shape
    q, hn = norm_matmul(x, norm_w, query_w.astype(BF16), BF16, emit_norm=True)
    keys = sub_keys.reshape(PEER_HEADS * 2, PEER_N_KEYS, -1).astype(BF16)
    idx, gate = peer_topk(q, keys)
    idx_flat = idx.reshape(t * PEER_PICKS)
    w = peer_u(idx_flat, hn.reshape(t * 8, 128), gate, pack_expert_table(expert_u))
    y = peer_v(idx_flat, w, x.reshape(t * 8, 128), pack_expert_table(expert_v))
    return y.reshape(t, d)


def ssd_layer(x, norm_w, in_w, conv_w, conv_b, dt_bias, a_log, d_skip, gnorm_w, out_w, bsz, seq):
    n_zx = SSD_D_INNER + conv_w.shape[1]
    zx = norm_matmul(x, norm_w, in_w[:, :n_zx].astype(BF16), BF16)
    w_dt = jnp.pad(in_w[:, n_zx:], ((0, 0), (0, 128 - SSD_N_HEADS))).astype(BF16)
    dt_raw = norm_matmul(x, norm_w, w_dt, F32)
    y = ssd_core(zx, dt_raw, conv_w, conv_b, dt_bias, a_log, d_skip, gnorm_w, bsz, seq)
    return matmul_res(y, out_w.astype(BF16), x)


def attn_layer(x, norm_w, qkv_w, q_norm_w, k_norm_w, lam_q1, lam_k1, lam_q2, lam_k2, subln_w, out_w,
               lambda_init, bsz, seq):
    d = x.shape[1]
    reps = d // DA_HEAD_DIM
    head_w = jnp.concatenate([jnp.tile(q_norm_w, reps) * (DA_HEAD_DIM ** -0.5), jnp.tile(k_norm_w, reps),
                              jnp.ones((d,), F32)])
    qkv = norm_matmul_qk(x, norm_w, qkv_w.astype(BF16), head_w, 2 * d)
    lam_params = jnp.stack([lam_q1, lam_k1, lam_q2, lam_k2]).astype(F32)
    o = diff_attention_core(qkv, lam_params, subln_w, lambda_init, bsz, seq)
    return matmul_res(o, out_w.astype(BF16), x)


def kernel(x, mix_norm_w, ffn_norm_w, ssd_in_w, ssd_conv_w, ssd_conv_b, ssd_dt_bias, ssd_a_log, ssd_d,
           ssd_norm_w, ssd_out_w, da_qkv_w, da_q_norm_w, da_k_norm_w, da_lam_q1, da_lam_k1, da_lam_q2,
           da_lam_k2, da_subln_w, da_out_w, peer_query_w, peer_sub_keys, peer_u, peer_v):
    bsz, seq, d = x.shape
    depth = mix_norm_w.shape[0]
    h = x.reshape(bsz * seq, d)
    for i in range(depth):
        j = i // 2
        if i % 2 == 0:
            h = ssd_layer(h, mix_norm_w[i], ssd_in_w[j], ssd_conv_w[j], ssd_conv_b[j], ssd_dt_bias[j],
                          ssd_a_log[j], ssd_d[j], ssd_norm_w[j], ssd_out_w[j], bsz, seq)
        else:
            lambda_init = 0.8 - 0.6 * math.exp(-0.3 * i)
            h = attn_layer(h, mix_norm_w[i], da_qkv_w[j], da_q_norm_w[j], da_k_norm_w[j], da_lam_q1[j],
                           da_lam_k1[j], da_lam_q2[j], da_lam_k2[j], da_subln_w[j], da_out_w[j],
                           lambda_init, bsz, seq)
        h = peer_block(h, ffn_norm_w[i], peer_query_w[i], peer_sub_keys[i], peer_u[i], peer_v[i])
    return h.reshape(bsz, seq, d)
```

```python
import functools
import math

import jax
import jax.numpy as jnp
from jax import lax
from jax.experimental import pallas as pl
from jax.experimental.pallas import tpu as pltpu

F32 = jnp.float32
BF16 = jnp.bfloat16
I32 = jnp.int32
U32 = jnp.uint32

RMS_EPS = 1e-6
CHUNK = 64

SSD_HEAD_DIM = 64
SSD_N_HEADS = 32
SSD_N_GROUPS = 8
SSD_D_STATE = 128
SSD_D_INNER = 2048
SSD_CONV = 4
SSD_GROUP_W = SSD_D_INNER // SSD_N_GROUPS

DA_N_HEADS = 8
DA_HEAD_DIM = 64

PEER_N_KEYS = 128
PEER_HEADS = 8
PEER_TOPK = 16
PEER_PICKS = PEER_HEADS * PEER_TOPK
PEER_ROWS_PER_EXPERT = 4

VMEM_LIMIT = 56 * 1024 * 1024

NT_DIMS = (((1,), (1,)), ((), ()))


def _cparams(sem):
    return pltpu.CompilerParams(dimension_semantics=sem, vmem_limit_bytes=VMEM_LIMIT)


def _norm_matmul_kernel(x_ref, nw_ref, w_ref, o_ref, *rest, emit_norm):
    if emit_norm:
        hn_ref, xn_ref = rest
    else:
        (xn_ref,) = rest
    j = pl.program_id(1)

    @pl.when(j == 0)
    def _():
        x = x_ref[...]
        ms = jnp.mean(x * x, axis=-1, keepdims=True)
        xn = x * lax.rsqrt(ms + RMS_EPS) * nw_ref[...]
        xn_ref[...] = xn.astype(BF16)
        if emit_norm:
            hn_ref[...] = xn.astype(BF16)

    o_ref[...] = jnp.dot(xn_ref[...], w_ref[...], preferred_element_type=F32).astype(o_ref.dtype)


def norm_matmul(x, nw, w_bf16, out_dtype, *, tm=1024, tn=1024, emit_norm=False):
    t, d = x.shape
    n = w_bf16.shape[1]
    tn = min(tn, n)
    assert t % tm == 0 and n % tn == 0
    out_shape = [jax.ShapeDtypeStruct((t, n), out_dtype)]
    out_specs = [pl.BlockSpec((tm, tn), lambda i, j: (i, j))]
    if emit_norm:
        out_shape.append(jax.ShapeDtypeStruct((t, d), BF16))
        out_specs.append(pl.BlockSpec((tm, d), lambda i, j: (i, 0)))
    res = pl.pallas_call(
        functools.partial(_norm_matmul_kernel, emit_norm=emit_norm),
        grid=(t // tm, n // tn),
        in_specs=[pl.BlockSpec((tm, d), lambda i, j: (i, 0)),
                  pl.BlockSpec((1, d), lambda i, j: (0, 0)),
                  pl.BlockSpec((d, tn), lambda i, j: (0, j))],
        out_specs=out_specs,
        out_shape=out_shape,
        scratch_shapes=[pltpu.VMEM((tm, d), BF16)],
        compiler_params=_cparams(("parallel", "arbitrary")),
        name="norm_matmul",
    )(x, nw.reshape(1, d), w_bf16)
    return res if emit_norm else res[0]


def _norm_matmul_qk_kernel(x_ref, nw_ref, w_ref, seg_ref, hw_ref, o_ref, xn_ref, *, n_norm_blocks):
    j = pl.program_id(1)

    @pl.when(j == 0)
    def _():
        x = x_ref[...]
        ms = jnp.mean(x * x, axis=-1, keepdims=True)
        xn_ref[...] = (x * lax.rsqrt(ms + RMS_EPS) * nw_ref[...]).astype(BF16)

    acc = jnp.dot(xn_ref[...], w_ref[...], preferred_element_type=F32)

    @pl.when(j < n_norm_blocks)
    def _():
        ss = jnp.dot((acc * acc).astype(BF16), seg_ref[...], preferred_element_type=F32)
        o_ref[...] = (acc * lax.rsqrt(ss * (1.0 / DA_HEAD_DIM) + RMS_EPS) * hw_ref[...]).astype(o_ref.dtype)

    @pl.when(j >= n_norm_blocks)
    def _():
        o_ref[...] = acc.astype(o_ref.dtype)


def norm_matmul_qk(x, nw, w_bf16, head_w, n_norm_cols, *, tm=1024, tn=512):
    t, d = x.shape
    n = w_bf16.shape[1]
    assert t % tm == 0 and n % tn == 0 and n_norm_cols % tn == 0
    lane = jnp.arange(tn)
    seg = (lane[:, None] // DA_HEAD_DIM == lane[None, :] // DA_HEAD_DIM).astype(BF16)
    return pl.pallas_call(
        functools.partial(_norm_matmul_qk_kernel, n_norm_blocks=n_norm_cols // tn),
        grid=(t // tm, n // tn),
        in_specs=[pl.BlockSpec((tm, d), lambda i, j: (i, 0)),
                  pl.BlockSpec((1, d), lambda i, j: (0, 0)),
                  pl.BlockSpec((d, tn), lambda i, j: (0, j)),
                  pl.BlockSpec((tn, tn), lambda i, j: (0, 0)),
                  pl.BlockSpec((1, tn), lambda i, j: (0, j))],
        out_specs=pl.BlockSpec((tm, tn), lambda i, j: (i, j)),
        out_shape=jax.ShapeDtypeStruct((t, n), BF16),
        scratch_shapes=[pltpu.VMEM((tm, d), BF16)],
        compiler_params=_cparams(("parallel", "arbitrary")),
        name="norm_matmul_qk",
    )(x, nw.reshape(1, d), w_bf16, seg, head_w.reshape(1, n))


def _matmul_res_kernel(a_ref, w_ref, r_ref, o_ref):
    o_ref[...] = r_ref[...] + jnp.dot(a_ref[...], w_ref[...], preferred_element_type=F32)


def matmul_res(a_bf16, w_bf16, res, *, tm=1024):
    t, k = a_bf16.shape
    n = w_bf16.shape[1]
    assert t % tm == 0
    return pl.pallas_call(
        _matmul_res_kernel,
        grid=(t // tm,),
        in_specs=[pl.BlockSpec((tm, k), lambda i: (i, 0)),
                  pl.BlockSpec((k, n), lambda i: (0, 0)),
                  pl.BlockSpec((tm, n), lambda i: (i, 0))],
        out_specs=pl.BlockSpec((tm, n), lambda i: (i, 0)),
        out_shape=jax.ShapeDtypeStruct((t, n), F32),
        compiler_params=_cparams(("parallel",)),
        name="matmul_res",
    )(a_bf16, w_bf16, res)


def _silu(v):
    return v * jax.nn.sigmoid(v)


def _ssd_kernel(z_ref, xs_ref, bc_ref, dt_ref, cwx_ref, cbx_ref, cwb_ref, cbb_ref, dtb_ref, alog_ref,
                expand_ref, dskip_ref, nw_ref, o_ref,
                state_ref, xtail_ref, btail_ref, xpad_ref, bpad_ref):
    c = pl.program_id(1)
    lc = CHUNK

    @pl.when(c == 0)
    def _():
        state_ref[...] = jnp.zeros_like(state_ref)
        xtail_ref[...] = jnp.zeros_like(xtail_ref)
        btail_ref[...] = jnp.zeros_like(btail_ref)

    def conv(x_ref, tail_ref, pad_ref, w_ref, b_ref):
        xin = x_ref[...].astype(F32)
        pad_ref[0:8, :] = tail_ref[...]
        pad_ref[8:8 + lc, :] = xin
        tail_ref[...] = xin[lc - 8:lc, :]
        acc = jnp.broadcast_to(b_ref[...], xin.shape)
        for k in range(SSD_CONV):
            off = 8 - (SSD_CONV - 1) + k
            acc = acc + w_ref[k:k + 1, :] * pad_ref[off:off + lc, :]
        return _silu(acc)

    xs = conv(xs_ref, xtail_ref, xpad_ref, cwx_ref, cbx_ref)
    bc = conv(bc_ref, btail_ref, bpad_ref, cwb_ref, cbb_ref)

    dt = jax.nn.softplus(dt_ref[...] + dtb_ref[...])
    adt = dt * (-jnp.exp(alog_ref[...]))
    r = lax.broadcasted_iota(I32, (lc, lc), 0)
    s = lax.broadcasted_iota(I32, (lc, lc), 1)
    tril = (r >= s).astype(F32)
    a_cs = jnp.dot(tril, adt, preferred_element_type=F32, precision=lax.Precision.HIGHEST)
    both = jnp.concatenate([dt, a_cs], axis=0)
    both_x = jnp.dot(both, expand_ref[...], preferred_element_type=F32,
                     precision=lax.Precision.HIGHEST)
    dt_x = both_x[0:lc, :]
    acs_x = both_x[lc:2 * lc, :]
    rr = lax.broadcasted_iota(I32, (lc, SSD_D_INNER), 0)
    ll = lax.broadcasted_iota(I32, (lc, SSD_D_INNER), 1)
    rowvec = jnp.sum(jnp.where((ll % lc) == rr, acs_x, 0.0), axis=0, keepdims=True)
    alast_x = acs_x[lc - 1:lc, :]

    lrow = lax.broadcasted_iota(I32, (lc, 128), 0)
    llane = lax.broadcasted_iota(I32, (lc, 128), 1)
    causal2 = (llane % lc) <= lrow
    brow = lax.broadcasted_iota(I32, (128, 128), 0)
    blane = lax.broadcasted_iota(I32, (128, 128), 1)
    blockdiag = (brow // lc) == (blane // lc)

    for g in range(SSD_N_GROUPS):
        lo = g * SSD_GROUP_W
        xs_g = xs[:, lo:lo + SSD_GROUP_W]
        b_g = bc[:, g * SSD_D_STATE:(g + 1) * SSD_D_STATE]
        c_g = bc[:, 1024 + g * SSD_D_STATE:1024 + (g + 1) * SSD_D_STATE]
        dt_g = dt_x[:, lo:lo + SSD_GROUP_W]
        acs_g = acs_x[:, lo:lo + SSD_GROUP_W]
        row_g = rowvec[:, lo:lo + SSD_GROUP_W]
        alast_g = alast_x[:, lo:lo + SSD_GROUP_W]

        xd_g = xs_g * dt_g
        cb = c_g.astype(BF16)
        bb = b_g.astype(BF16)
        st = state_ref[g]
        y_off = jnp.dot(cb, st.astype(BF16), preferred_element_type=F32) * jnp.exp(acs_g)
        b2 = jnp.concatenate([bb, bb], axis=0)
        cb2 = lax.dot_general(cb, b2, NT_DIMS, preferred_element_type=F32)
        yd = []
        for q in range(2):
            ql = q * 128
            seg = acs_g[:, ql:ql + 128] - row_g[:, ql:ql + 128]
            m = cb2 * jnp.exp(jnp.where(causal2, seg, -jnp.inf))
            xp = xd_g[:, ql:ql + 128].astype(BF16)
            x2 = jnp.concatenate([xp, xp], axis=0)
            x2 = jnp.where(blockdiag, x2, jnp.zeros_like(x2))
            yd.append(jnp.dot(m.astype(BF16), x2, preferred_element_type=F32))
        y_diag = jnp.concatenate(yd, axis=-1)

        xdw = (xd_g * jnp.exp(alast_g - acs_g)).astype(BF16)
        upd = jnp.dot(b_g.T.astype(BF16), xdw, preferred_element_type=F32)
        state_ref[g] = st * jnp.exp(alast_g) + upd

        y_g = y_diag + y_off + xs_g * dskip_ref[:, lo:lo + SSD_GROUP_W]
        y_g = y_g * _silu(z_ref[:, lo:lo + SSD_GROUP_W].astype(F32))
        ms = jnp.mean(y_g * y_g, axis=-1, keepdims=True)
        y_g = y_g * lax.rsqrt(ms + RMS_EPS) * nw_ref[:, lo:lo + SSD_GROUP_W]
        o_ref[:, lo:lo + SSD_GROUP_W] = y_g.astype(o_ref.dtype)


def ssd_core(zx, dt_raw, conv_w, conv_b, dt_bias, a_log, d_skip, norm_w, bsz, seq):
    t = zx.shape[0]
    nc = seq // CHUNK
    w = SSD_D_INNER
    pad = 128 - SSD_N_HEADS
    head = jnp.arange(128)
    expand = (head[:, None] == (jnp.arange(w)[None, :] // SSD_HEAD_DIM)).astype(F32)
    row = lambda v: v.reshape(1, -1).astype(F32)
    args = (zx, zx, zx, dt_raw,
            conv_w[:, :w], row(conv_b[:w]), conv_w[:, w:], row(conv_b[w:]),
            row(jnp.pad(dt_bias, (0, pad))), row(jnp.pad(a_log, (0, pad))),
            expand, row(jnp.repeat(d_skip, SSD_HEAD_DIM)), row(norm_w))
    const = lambda shape: pl.BlockSpec(shape, lambda b, c: (0, 0))
    return pl.pallas_call(
        _ssd_kernel,
        grid=(bsz, nc),
        in_specs=[pl.BlockSpec((CHUNK, w), lambda b, c: (b * nc + c, 0)),
                  pl.BlockSpec((CHUNK, w), lambda b, c: (b * nc + c, 1)),
                  pl.BlockSpec((CHUNK, w), lambda b, c: (b * nc + c, 2)),
                  pl.BlockSpec((CHUNK, 128), lambda b, c: (b * nc + c, 0)),
                  const((SSD_CONV, w)), const((1, w)), const((SSD_CONV, w)), const((1, w)),
                  const((1, 128)), const((1, 128)), const((128, w)), const((1, w)), const((1, w))],
        out_specs=pl.BlockSpec((CHUNK, w), lambda b, c: (b * nc + c, 0)),
        out_shape=jax.ShapeDtypeStruct((t, w), BF16),
        scratch_shapes=[pltpu.VMEM((SSD_N_GROUPS, SSD_D_STATE, SSD_GROUP_W), F32),
                        pltpu.VMEM((8, w), F32), pltpu.VMEM((8, w), F32),
                        pltpu.VMEM((8 + CHUNK, w), F32), pltpu.VMEM((8 + CHUNK, w), F32)],
        compiler_params=_cparams(("parallel", "arbitrary")),
        name="ssd_core",
    )(*args)


DA_VT_ROWS = 2 * DA_HEAD_DIM + 16


def _attn_kernel(q_ref, k_ref, vt_ref, lamp_ref, sw_ref, o_ref, m_ref, a_ref, s0_ref, s1_ref, b0_ref, b1_ref,
                 *, tq, tk, lambda_init):
    i = pl.program_id(2)
    hw = 2 * DA_HEAD_DIM
    q = q_ref[...]
    lane = lax.broadcasted_iota(I32, q.shape, 1)
    zero = jnp.zeros_like(q)
    qs = (jnp.where(lane < DA_HEAD_DIM, q, zero), jnp.where(lane >= DA_HEAD_DIM, q, zero))
    m_ref[...] = jnp.full_like(m_ref, -jnp.inf)
    a_ref[...] = jnp.zeros_like(a_ref)
    slots = ((s0_ref, b0_ref), (s1_ref, b1_ref))

    def qk(j, slot, masked):
        st_ref, mb_ref = slots[slot]
        kb = k_ref[pl.ds(pl.multiple_of(j * tk, tk), tk), :]
        if masked:
            kc = lax.broadcasted_iota(I32, (tk, tq), 0) // CHUNK
            qc = lax.broadcasted_iota(I32, (tk, tq), 1) // CHUNK
            visible = kc <= qc
        for mi in range(2):
            st = lax.dot_general(kb, qs[mi], NT_DIMS, preferred_element_type=F32)
            if masked:
                st = jnp.where(visible, st, -jnp.inf)
            st_ref[mi] = st
            mb_ref[mi] = jnp.max(st, axis=0, keepdims=True)

    def pv(j, slot):
        st_ref, mb_ref = slots[slot]
        vt = vt_ref[j]
        for mi in range(2):
            m_prev = m_ref[mi]
            m_next = jnp.maximum(m_prev, mb_ref[mi])
            p = jnp.exp((st_ref[mi] - m_next).astype(BF16))
            alpha = jnp.exp(m_prev - m_next)
            a_ref[mi] = alpha * a_ref[mi] + jnp.dot(vt, p, preferred_element_type=F32)
            m_ref[mi] = m_next

    @pl.when(i == 0)
    def _():
        qk(0, 0, True)
        pv(0, 0)

    @pl.when(i > 0)
    def _():
        qk(0, 0, False)
        n_pairs = lax.shift_right_logical(i - 1, 1)

        def body(jj, carry):
            j = 2 * jj
            qk(j + 1, 1, False)
            pv(j, 0)
            qk(j + 2, 0, False)
            pv(j + 1, 1)
            return carry

        lax.fori_loop(0, n_pairs, body, 0)

        @pl.when(i % 2 == 1)
        def _():
            qk(i, 1, True)
            pv(i - 1, 0)
            pv(i, 1)

        @pl.when(i % 2 == 0)
        def _():
            qk(i - 1, 1, False)
            pv(i - 2, 0)
            qk(i, 0, True)
            pv(i - 1, 1)
            pv(i, 0)

    lp = lamp_ref[...]
    lam = (jnp.exp(jnp.sum(lp[0:1] * lp[1:2], axis=-1, keepdims=True))
           - jnp.exp(jnp.sum(lp[2:3] * lp[3:4], axis=-1, keepdims=True)) + lambda_init)
    a0 = a_ref[0]
    a1 = a_ref[1]
    ot = a0[0:hw] / a0[hw:hw + 1] - lam * (a1[0:hw] / a1[hw:hw + 1])
    ms = jnp.mean(ot * ot, axis=0, keepdims=True)
    ot = ot * lax.rsqrt(ms + RMS_EPS) * (sw_ref[...] * (1.0 - lambda_init))
    o_ref[...] = ot.T.astype(o_ref.dtype)


def diff_attention_core(qkv, lam_params, subln_w, lambda_init, bsz, seq, *, tq=512):
    t = qkv.shape[0]
    tk = tq
    nq = seq // tq
    nh = DA_N_HEADS
    hw = 2 * DA_HEAD_DIM
    v = qkv[:, 2 * nh * hw:].reshape(bsz, nq, tk, nh, hw)
    vt = jnp.transpose(v, (0, 3, 1, 4, 2))
    vt = jnp.concatenate([vt, jnp.ones((bsz, nh, nq, DA_VT_ROWS - hw, tk), BF16)], axis=3)
    vt = vt.reshape(bsz * nh * nq, DA_VT_ROWS, tk)
    return pl.pallas_call(
        functools.partial(_attn_kernel, tq=tq, tk=tk, lambda_init=lambda_init),
        grid=(bsz, nh, nq),
        in_specs=[pl.BlockSpec((tq, hw), lambda b, h, i: (b * nq + i, h)),
                  pl.BlockSpec((seq, hw), lambda b, h, i: (b, nh + h)),
                  pl.BlockSpec((nq, DA_VT_ROWS, tk), lambda b, h, i: (b * nh + h, 0, 0)),
                  pl.BlockSpec((4, DA_HEAD_DIM), lambda b, h, i: (0, 0)),
                  pl.BlockSpec((hw, 1), lambda b, h, i: (0, 0))],
        out_specs=pl.BlockSpec((tq, hw), lambda b, h, i: (b * nq + i, h)),
        out_shape=jax.ShapeDtypeStruct((t, nh * hw), BF16),
        scratch_shapes=[pltpu.VMEM((2, 1, tq), F32), pltpu.VMEM((2, DA_VT_ROWS, tq), F32),
                        pltpu.VMEM((2, tk, tq), F32), pltpu.VMEM((2, tk, tq), F32),
                        pltpu.VMEM((2, 1, tq), F32), pltpu.VMEM((2, 1, tq), F32)],
        compiler_params=_cparams(("parallel", "parallel", "arbitrary")),
        name="diff_attention",
    )(qkv, qkv, vt, lam_params, subln_w.reshape(hw, 1))


def _candidate_blocks():
    kk = PEER_TOPK
    blocks = [(0, 1, 0, 8), (0, 1, 8, 8)]
    blocks += [(i, 1, 0, min(8, kk // (i + 1))) for i in range(1, 8)]
    blocks.append((8, 8, 0, 1))
    return blocks


def _peer_topk_kernel(q_ref, keys_ref, idx_ref, gate_ref, ts_ref, ti_ref, bs_ref, bp_ref, bi_ref, bg_ref, *, tt):
    kk = PEER_TOPK
    nk = PEER_N_KEYS
    big = jnp.float32(1e9)
    neg_inf = jnp.float32(-jnp.inf)
    key_iota = lax.broadcasted_iota(I32, (nk, tt), 0).astype(F32)
    r8 = lax.broadcasted_iota(I32, (8, tt), 0).astype(F32)
    blocks = _candidate_blocks()
    pos = jnp.concatenate(
        [jnp.where(r8 < max(n_i, n_j), (i0 * kk + j0) + r8 * (kk if n_i > 1 else 1), big)
         for i0, n_i, j0, n_j in blocks], axis=0)
    for h in range(PEER_HEADS):
        for c in range(2):
            hc = 2 * h + c
            s = lax.dot_general(keys_ref[hc], q_ref[:, hc * 128:(hc + 1) * 128], NT_DIMS,
                                preferred_element_type=F32)
            for k in range(kk):
                m = jnp.max(s, axis=0, keepdims=True)
                first = jnp.min(jnp.where(s == m, key_iota, big), axis=0, keepdims=True)
                s = jnp.where(key_iota == first, neg_inf, s)
                ts_ref[c, k:k + 1, :] = m
                ti_ref[c, k:k + 1, :] = first
        cand = []
        for i0, n_i, j0, n_j in blocks:
            a = ts_ref[0, i0:i0 + n_i, :]
            b = ts_ref[1, j0:j0 + 8, :] if n_j > 1 else ts_ref[1, j0:j0 + 1, :]
            cand.append(jnp.where(r8 < max(n_i, n_j), a + b, neg_inf))
        cand = jnp.concatenate(cand, axis=0)
        for k in range(kk):
            m = jnp.max(cand, axis=0, keepdims=True)
            first = jnp.min(jnp.where(cand == m, pos, big), axis=0, keepdims=True)
            cand = jnp.where(pos == first, neg_inf, cand)
            bs_ref[k:k + 1, :] = m
            bp_ref[k:k + 1, :] = first
        p = bp_ref[...]
        pi = jnp.floor(p * (1.0 / kk))
        pj = p - pi * kk
        e0 = jnp.zeros_like(p)
        e1 = jnp.zeros_like(p)
        for r in range(kk):
            e0 = e0 + jnp.where(pi == r, ti_ref[0, r:r + 1, :], 0.0)
            e1 = e1 + jnp.where(pj == r, ti_ref[1, r:r + 1, :], 0.0)
        rows = (e0 * nk + e1) * PEER_ROWS_PER_EXPERT
        bi_ref[h * kk:(h + 1) * kk, :] = rows.astype(I32)
        bs = bs_ref[...]
        e = jnp.exp(bs - bs[0:1])
        bg_ref[h * kk:(h + 1) * kk, :] = e / jnp.sum(e, axis=0, keepdims=True)
    idx_ref[...] = bi_ref[...].T
    gate_ref[...] = bg_ref[...].T


def peer_topk(q_bf16, keys_bf16, *, tt=128):
    t = q_bf16.shape[0]
    assert t % tt == 0
    return pl.pallas_call(
        functools.partial(_peer_topk_kernel, tt=tt),
        grid=(t // tt,),
        in_specs=[pl.BlockSpec((tt, q_bf16.shape[1]), lambda i: (i, 0)),
                  pl.BlockSpec(keys_bf16.shape, lambda i: (0, 0, 0))],
        out_specs=[pl.BlockSpec((tt, PEER_PICKS), lambda i: (i, 0)),
                   pl.BlockSpec((tt, PEER_PICKS), lambda i: (i, 0))],
        out_shape=[jax.ShapeDtypeStruct((t, PEER_PICKS), I32),
                   jax.ShapeDtypeStruct((t, PEER_PICKS), F32)],
        scratch_shapes=[pltpu.VMEM((2, PEER_TOPK, tt), F32), pltpu.VMEM((2, PEER_TOPK, tt), F32),
                        pltpu.VMEM((PEER_TOPK, tt), F32), pltpu.VMEM((PEER_TOPK, tt), F32),
                        pltpu.VMEM((PEER_PICKS, tt), I32), pltpu.VMEM((PEER_PICKS, tt), F32)],
        compiler_params=_cparams(("parallel",)),
        name="peer_topk",
    )(q_bf16, keys_bf16)


def pack_expert_table(tab):
    n, d = tab.shape
    tb = tab.astype(BF16).reshape(n, d // 256, 2, 128)
    lo = lax.bitcast_convert_type(tb[:, :, 0, :], jnp.uint16).astype(U32)
    hi = lax.bitcast_convert_type(tb[:, :, 1, :], jnp.uint16).astype(U32)
    return (lo | (hi << 16)).reshape(n * (d // 256), 128)


PEER_TOKEN_GROUP = 2


def _gather_tokens(idx_ref, tab_ref, buf_refs, tok0):
    r = PEER_ROWS_PER_EXPERT
    tok_idx = [idx_ref.at[pl.ds((tok0 + g) * PEER_PICKS, PEER_PICKS)] for g in range(len(buf_refs))]
    for k in range(PEER_PICKS):
        for ids, buf_ref in zip(tok_idx, buf_refs):
            row = pl.multiple_of(ids[k], r)
            buf_ref[k * r:(k + 1) * r, :] = tab_ref[pl.ds(row, r), :]


def _gathered(buf_ref):
    return pltpu.bitcast(buf_ref[...], BF16)


def _pipelined_tokens(tt, gather, compute, bufs):
    g = PEER_TOKEN_GROUP
    set_a, set_b = bufs[:g], bufs[g:]
    assert tt % (2 * g) == 0
    gather(set_b, 0)

    def octet(t, last):
        for n, buf in enumerate(set_b):
            compute(t + n, buf)
        gather(set_a, t + g)
        for n, buf in enumerate(set_a):
            compute(t + g + n, buf)
        if not last:
            gather(set_b, t + 2 * g)

    def body(it, carry):
        octet(2 * g * it, False)
        return carry

    lax.fori_loop(0, tt // (2 * g) - 1, body, 0)
    octet(tt - 2 * g, True)


def _segment_mask(rows, cols):
    sub = lax.broadcasted_iota(I32, (rows, cols), 0)
    lane = lax.broadcasted_iota(I32, (rows, cols), 1)
    return (lane % rows) == sub


def _peer_u_kernel(idx_ref, hn_ref, gate_ref, tab_ref, fold_ref, w_ref, *scratch, tt):
    *bufs, r_ref, x8_ref = scratch
    mask = _segment_mask(8, 8 * PEER_PICKS)
    for m in range(8):
        x8_ref[pl.ds(m, tt, stride=8), :] = hn_ref[:, m * 128:(m + 1) * 128].astype(F32)

    def gather(buf_refs, tok0):
        _gather_tokens(idx_ref, tab_ref, buf_refs, tok0)

    def compute(tok, buf_ref):
        xm = x8_ref[pl.ds(pl.multiple_of(tok * 8, 8), 8), :].astype(BF16)
        y = lax.dot_general(xm, _gathered(buf_ref), NT_DIMS, preferred_element_type=F32)
        r_ref[pl.ds(tok, 1), :] = jnp.sum(jnp.where(mask, y, 0.0), axis=0, keepdims=True)

    _pipelined_tokens(tt, gather, compute, bufs)
    act = jnp.dot(r_ref[...], fold_ref[...], preferred_element_type=F32, precision=lax.Precision.HIGHEST)
    w_ref[...] = gate_ref[...] * (0.5 * act * (1.0 + lax.erf(act * (2.0 ** -0.5))))


def _peer_v_kernel(idx_ref, w_ref, x_ref, tab_ref, spread_ref, o_ref, *scratch, tt):
    *bufs, wx_ref, o8_ref = scratch
    mask = _segment_mask(8, 8 * PEER_PICKS)
    wx_ref[...] = jnp.dot(w_ref[...].astype(BF16), spread_ref[...], preferred_element_type=F32)

    def gather(buf_refs, tok0):
        _gather_tokens(idx_ref, tab_ref, buf_refs, tok0)

    def compute(tok, buf_ref):
        wrow = jnp.broadcast_to(wx_ref[pl.ds(tok, 1), :], mask.shape)
        wsel = jnp.where(mask, wrow, 0.0).astype(BF16)
        rows = pl.ds(pl.multiple_of(tok * 8, 8), 8)
        o8_ref[rows, :] = jnp.dot(wsel, _gathered(buf_ref), preferred_element_type=F32)

    _pipelined_tokens(tt, gather, compute, bufs)
    for m in range(8):
        seg = slice(m * 128, (m + 1) * 128)
        o_ref[:, seg] = x_ref[:, seg] + o8_ref[pl.ds(m, tt, stride=8), :]


def _table_spec(shape):
    return pl.BlockSpec(shape, lambda i: (0, 0), pipeline_mode=pl.Buffered(1))


def peer_u(idx, hn, gate, utab, *, tt=64):
    t = gate.shape[0]
    assert t % tt == 0
    pick = jnp.arange(8 * PEER_PICKS) // 8
    fold = (pick[:, None] == jnp.arange(PEER_PICKS)[None, :]).astype(F32)
    gbuf = lambda: pltpu.VMEM((PEER_PICKS * PEER_ROWS_PER_EXPERT, 128), U32)
    return pl.pallas_call(
        functools.partial(_peer_u_kernel, tt=tt),
        grid=(t // tt,),
        in_specs=[pl.BlockSpec((tt * PEER_PICKS,), lambda i: (i,), memory_space=pltpu.SMEM),
                  pl.BlockSpec((tt, hn.shape[1]), lambda i: (i, 0)),
                  pl.BlockSpec((tt, PEER_PICKS), lambda i: (i, 0)),
                  _table_spec(utab.shape),
                  pl.BlockSpec(fold.shape, lambda i: (0, 0))],
        out_specs=pl.BlockSpec((tt, PEER_PICKS), lambda i: (i, 0)),
        out_shape=jax.ShapeDtypeStruct((t, PEER_PICKS), F32),
        scratch_shapes=[gbuf() for _ in range(2 * PEER_TOKEN_GROUP)]
        + [pltpu.VMEM((tt, 8 * PEER_PICKS), F32), pltpu.VMEM((tt * 8, 128), F32)],
        compiler_params=_cparams(("arbitrary",)),
        name="peer_u",
    )(idx, hn, gate, utab, fold)


def peer_v(idx, w, x, vtab, *, tt=64):
    t = w.shape[0]
    assert t % tt == 0
    pick = jnp.arange(8 * PEER_PICKS) // 8
    spread = (jnp.arange(PEER_PICKS)[:, None] == pick[None, :]).astype(BF16)
    gbuf = lambda: pltpu.VMEM((PEER_PICKS * PEER_ROWS_PER_EXPERT, 128), U32)
    return pl.pallas_call(
        functools.partial(_peer_v_kernel, tt=tt),
        grid=(t // tt,),
        in_specs=[pl.BlockSpec((tt * PEER_PICKS,), lambda i: (i,), memory_space=pltpu.SMEM),
                  pl.BlockSpec((tt, PEER_PICKS), lambda i: (i, 0)),
                  pl.BlockSpec((tt, x.shape[1]), lambda i: (i, 0)),
                  _table_spec(vtab.shape),
                  pl.BlockSpec(spread.shape, lambda i: (0, 0))],
        out_specs=pl.BlockSpec((tt, x.shape[1]), lambda i: (i, 0)),
        out_shape=jax.ShapeDtypeStruct(x.shape, F32),
        scratch_shapes=[gbuf() for _ in range(2 * PEER_TOKEN_GROUP)]
        + [pltpu.VMEM((tt, 8 * PEER_PICKS), F32), pltpu.VMEM((tt * 8, 128), F32)],
        compiler_params=_cparams(("arbitrary",)),
        name="peer_v",
    )(idx, w, x, vtab, spread)


def peer_block(x, norm_w, query_w, sub_keys, expert_u, expert_v):
    t, d = x.shape
    q, hn = norm_matmul(x, norm_w, query_w.astype(BF16), BF16, emit_norm=True)
    keys = sub_keys.reshape(PEER_HEADS * 2, PEER_N_KEYS, -1).astype(BF16)
    idx, gate = peer_topk(q, keys)
    idx = idx.reshape(t * PEER_PICKS)
    w = peer_u(idx, hn, gate, pack_expert_table(expert_u))
    return peer_v(idx, w, x, pack_expert_table(expert_v))


def ssd_layer(x, norm_w, in_w, conv_w, conv_b, dt_bias, a_log, d_skip, gnorm_w, out_w, bsz, seq):
    n_zx = SSD_D_INNER + conv_w.shape[1]
    zx = norm_matmul(x, norm_w, in_w[:, :n_zx].astype(BF16), BF16)
    w_dt = jnp.pad(in_w[:, n_zx:], ((0, 0), (0, 128 - SSD_N_HEADS))).astype(BF16)
    dt_raw = norm_matmul(x, norm_w, w_dt, F32)
    y = ssd_core(zx, dt_raw, conv_w, conv_b, dt_bias, a_log, d_skip, gnorm_w, bsz, seq)
    return matmul_res(y, out_w.astype(BF16), x)


def attn_layer(x, norm_w, qkv_w, q_norm_w, k_norm_w, lam_q1, lam_k1, lam_q2, lam_k2, subln_w, out_w,
               lambda_init, bsz, seq):
    d = x.shape[1]
    reps = d // DA_HEAD_DIM
    head_w = jnp.concatenate([jnp.tile(q_norm_w, reps) * (DA_HEAD_DIM ** -0.5), jnp.tile(k_norm_w, reps),
                              jnp.ones((d,), F32)])
    qkv = norm_matmul_qk(x, norm_w, qkv_w.astype(BF16), head_w, 2 * d)
    lam_params = jnp.stack([lam_q1, lam_k1, lam_q2, lam_k2]).astype(F32)
    o = diff_attention_core(qkv, lam_params, subln_w, lambda_init, bsz, seq)
    return matmul_res(o, out_w.astype(BF16), x)


def kernel(x, mix_norm_w, ffn_norm_w, ssd_in_w, ssd_conv_w, ssd_conv_b, ssd_dt_bias, ssd_a_log, ssd_d,
           ssd_norm_w, ssd_out_w, da_qkv_w, da_q_norm_w, da_k_norm_w, da_lam_q1, da_lam_k1, da_lam_q2,
           da_lam_k2, da_subln_w, da_out_w, peer_query_w, peer_sub_keys, peer_u, peer_v):
    bsz, seq, d = x.shape
    depth = mix_norm_w.shape[0]
    h = x.reshape(bsz * seq, d)
    for i in range(depth):
        j = i // 2
        if i % 2 == 0:
            h = ssd_layer(h, mix_norm_w[i], ssd_in_w[j], ssd_conv_w[j], ssd_conv_b[j], ssd_dt_bias[j],
                          ssd_a_log[j], ssd_d[j], ssd_norm_w[j], ssd_out_w[j], bsz, seq)
        else:
            lambda_init = 0.8 - 0.6 * math.exp(-0.3 * i)
            h = attn_layer(h, mix_norm_w[i], da_qkv_w[j], da_q_norm_w[j], da_k_norm_w[j], da_lam_q1[j],
                           da_lam_k1[j], da_lam_q2[j], da_lam_k2[j], da_subln_w[j], da_out_w[j],
                           lambda_init, bsz, seq)
        h = peer_block(h, ffn_norm_w[i], peer_query_w[i], peer_sub_keys[i], peer_u[i], peer_v[i])
    return h.reshape(bsz, seq, d)
```

```python
import functools
import math

import jax
import jax.numpy as jnp
from jax import lax
from jax.experimental import pallas as pl
from jax.experimental.pallas import tpu as pltpu

F32 = jnp.float32
BF16 = jnp.bfloat16
I32 = jnp.int32
U32 = jnp.uint32

RMS_EPS = 1e-6
CHUNK = 64

SSD_HEAD_DIM = 64
SSD_N_HEADS = 32
SSD_N_GROUPS = 8
SSD_D_STATE = 128
SSD_D_INNER = 2048
SSD_CONV = 4
SSD_GROUP_W = SSD_D_INNER // SSD_N_GROUPS

DA_N_HEADS = 8
DA_HEAD_DIM = 64

PEER_N_KEYS = 128
PEER_HEADS = 8
PEER_TOPK = 16
PEER_PICKS = PEER_HEADS * PEER_TOPK
PEER_ROWS_PER_EXPERT = 4

VMEM_LIMIT = 56 * 1024 * 1024

NT_DIMS = (((1,), (1,)), ((), ()))


def _cparams(sem):
    return pltpu.CompilerParams(dimension_semantics=sem, vmem_limit_bytes=VMEM_LIMIT)


def _norm_matmul_kernel(x_ref, nw_ref, w_ref, o_ref, *rest, emit_norm):
    if emit_norm:
        hn_ref, xn_ref = rest
    else:
        (xn_ref,) = rest
    j = pl.program_id(1)

    @pl.when(j == 0)
    def _():
        x = x_ref[...]
        ms = jnp.mean(x * x, axis=-1, keepdims=True)
        xn = x * lax.rsqrt(ms + RMS_EPS) * nw_ref[...]
        xn_ref[...] = xn.astype(BF16)
        if emit_norm:
            hn_ref[...] = xn.astype(BF16)

    o_ref[...] = jnp.dot(xn_ref[...], w_ref[...], preferred_element_type=F32).astype(o_ref.dtype)


def norm_matmul(x, nw, w_bf16, out_dtype, *, tm=1024, tn=1024, emit_norm=False):
    t, d = x.shape
    n = w_bf16.shape[1]
    tn = min(tn, n)
    assert t % tm == 0 and n % tn == 0
    out_shape = [jax.ShapeDtypeStruct((t, n), out_dtype)]
    out_specs = [pl.BlockSpec((tm, tn), lambda i, j: (i, j))]
    if emit_norm:
        out_shape.append(jax.ShapeDtypeStruct((t, d), BF16))
        out_specs.append(pl.BlockSpec((tm, d), lambda i, j: (i, 0)))
    res = pl.pallas_call(
        functools.partial(_norm_matmul_kernel, emit_norm=emit_norm),
        grid=(t // tm, n // tn),
        in_specs=[pl.BlockSpec((tm, d), lambda i, j: (i, 0)),
                  pl.BlockSpec((1, d), lambda i, j: (0, 0)),
                  pl.BlockSpec((d, tn), lambda i, j: (0, j))],
        out_specs=out_specs,
        out_shape=out_shape,
        scratch_shapes=[pltpu.VMEM((tm, d), BF16)],
        compiler_params=_cparams(("parallel", "arbitrary")),
        name="norm_matmul",
    )(x, nw.reshape(1, d), w_bf16)
    return res if emit_norm else res[0]


def _norm_matmul_qk_kernel(x_ref, nw_ref, w_ref, seg_ref, hw_ref, o_ref, xn_ref, *, n_norm_blocks):
    j = pl.program_id(1)

    @pl.when(j == 0)
    def _():
        x = x_ref[...]
        ms = jnp.mean(x * x, axis=-1, keepdims=True)
        xn_ref[...] = (x * lax.rsqrt(ms + RMS_EPS) * nw_ref[...]).astype(BF16)

    acc = jnp.dot(xn_ref[...], w_ref[...], preferred_element_type=F32)

    @pl.when(j < n_norm_blocks)
    def _():
        ss = jnp.dot((acc * acc).astype(BF16), seg_ref[...], preferred_element_type=F32)
        o_ref[...] = (acc * lax.rsqrt(ss * (1.0 / DA_HEAD_DIM) + RMS_EPS) * hw_ref[...]).astype(o_ref.dtype)

    @pl.when(j >= n_norm_blocks)
    def _():
        o_ref[...] = acc.astype(o_ref.dtype)


def norm_matmul_qk(x, nw, w_bf16, head_w, n_norm_cols, *, tm=1024, tn=512):
    t, d = x.shape
    n = w_bf16.shape[1]
    assert t % tm == 0 and n % tn == 0 and n_norm_cols % tn == 0
    lane = jnp.arange(tn)
    seg = (lane[:, None] // DA_HEAD_DIM == lane[None, :] // DA_HEAD_DIM).astype(BF16)
    return pl.pallas_call(
        functools.partial(_norm_matmul_qk_kernel, n_norm_blocks=n_norm_cols // tn),
        grid=(t // tm, n // tn),
        in_specs=[pl.BlockSpec((tm, d), lambda i, j: (i, 0)),
                  pl.BlockSpec((1, d), lambda i, j: (0, 0)),
                  pl.BlockSpec((d, tn), lambda i, j: (0, j)),
                  pl.BlockSpec((tn, tn), lambda i, j: (0, 0)),
                  pl.BlockSpec((1, tn), lambda i, j: (0, j))],
        out_specs=pl.BlockSpec((tm, tn), lambda i, j: (i, j)),
        out_shape=jax.ShapeDtypeStruct((t, n), BF16),
        scratch_shapes=[pltpu.VMEM((tm, d), BF16)],
        compiler_params=_cparams(("parallel", "arbitrary")),
        name="norm_matmul_qk",
    )(x, nw.reshape(1, d), w_bf16, seg, head_w.reshape(1, n))


def _matmul_res_kernel(a_ref, w_ref, r_ref, o_ref):
    o_ref[...] = r_ref[...] + jnp.dot(a_ref[...], w_ref[...], preferred_element_type=F32)


def matmul_res(a_bf16, w_bf16, res, *, tm=1024):
    t, k = a_bf16.shape
    n = w_bf16.shape[1]
    assert t % tm == 0
    return pl.pallas_call(
        _matmul_res_kernel,
        grid=(t // tm,),
        in_specs=[pl.BlockSpec((tm, k), lambda i: (i, 0)),
                  pl.BlockSpec((k, n), lambda i: (0, 0)),
                  pl.BlockSpec((tm, n), lambda i: (i, 0))],
        out_specs=pl.BlockSpec((tm, n), lambda i: (i, 0)),
        out_shape=jax.ShapeDtypeStruct((t, n), F32),
        compiler_params=_cparams(("parallel",)),
        name="matmul_res",
    )(a_bf16, w_bf16, res)


def _silu(v):
    return v * jax.nn.sigmoid(v)


def _ssd_kernel(z_ref, xs_ref, bc_ref, dt_ref, cwx_ref, cbx_ref, cwb_ref, cbb_ref, dtb_ref, alog_ref,
                expand_ref, dskip_ref, nw_ref, o_ref,
                state_ref, xtail_ref, btail_ref, xpad_ref, bpad_ref):
    c = pl.program_id(1)
    lc = CHUNK

    @pl.when(c == 0)
    def _():
        state_ref[...] = jnp.zeros_like(state_ref)
        xtail_ref[...] = jnp.zeros_like(xtail_ref)
        btail_ref[...] = jnp.zeros_like(btail_ref)

    def conv(x_ref, tail_ref, pad_ref, w_ref, b_ref):
        xin = x_ref[...].astype(F32)
        pad_ref[0:8, :] = tail_ref[...]
        pad_ref[8:8 + lc, :] = xin
        tail_ref[...] = xin[lc - 8:lc, :]
        acc = jnp.broadcast_to(b_ref[...], xin.shape)
        for k in range(SSD_CONV):
            off = 8 - (SSD_CONV - 1) + k
            acc = acc + w_ref[k:k + 1, :] * pad_ref[off:off + lc, :]
        return _silu(acc)

    xs = conv(xs_ref, xtail_ref, xpad_ref, cwx_ref, cbx_ref)
    bc = conv(bc_ref, btail_ref, bpad_ref, cwb_ref, cbb_ref)

    dt = jax.nn.softplus(dt_ref[...] + dtb_ref[...])
    adt = dt * (-jnp.exp(alog_ref[...]))
    r = lax.broadcasted_iota(I32, (lc, lc), 0)
    s = lax.broadcasted_iota(I32, (lc, lc), 1)
    tril = (r >= s).astype(F32)
    a_cs = jnp.dot(tril, adt, preferred_element_type=F32, precision=lax.Precision.HIGHEST)
    both = jnp.concatenate([dt, a_cs], axis=0)
    both_x = jnp.dot(both, expand_ref[...], preferred_element_type=F32,
                     precision=lax.Precision.HIGHEST)
    dt_x = both_x[0:lc, :]
    acs_x = both_x[lc:2 * lc, :]
    rr = lax.broadcasted_iota(I32, (lc, SSD_D_INNER), 0)
    ll = lax.broadcasted_iota(I32, (lc, SSD_D_INNER), 1)
    rowvec = jnp.sum(jnp.where((ll % lc) == rr, acs_x, 0.0), axis=0, keepdims=True)
    alast_x = acs_x[lc - 1:lc, :]

    lrow = lax.broadcasted_iota(I32, (lc, 128), 0)
    llane = lax.broadcasted_iota(I32, (lc, 128), 1)
    causal2 = (llane % lc) <= lrow
    brow = lax.broadcasted_iota(I32, (128, 128), 0)
    blane = lax.broadcasted_iota(I32, (128, 128), 1)
    blockdiag = (brow // lc) == (blane // lc)

    for g in range(SSD_N_GROUPS):
        lo = g * SSD_GROUP_W
        xs_g = xs[:, lo:lo + SSD_GROUP_W]
        b_g = bc[:, g * SSD_D_STATE:(g + 1) * SSD_D_STATE]
        c_g = bc[:, 1024 + g * SSD_D_STATE:1024 + (g + 1) * SSD_D_STATE]
        dt_g = dt_x[:, lo:lo + SSD_GROUP_W]
        acs_g = acs_x[:, lo:lo + SSD_GROUP_W]
        row_g = rowvec[:, lo:lo + SSD_GROUP_W]
        alast_g = alast_x[:, lo:lo + SSD_GROUP_W]

        xd_g = xs_g * dt_g
        cb = c_g.astype(BF16)
        bb = b_g.astype(BF16)
        st = state_ref[g]
        y_off = jnp.dot(cb, st.astype(BF16), preferred_element_type=F32) * jnp.exp(acs_g)
        b2 = jnp.concatenate([bb, bb], axis=0)
        cb2 = lax.dot_general(cb, b2, NT_DIMS, preferred_element_type=F32)
        yd = []
        for q in range(2):
            ql = q * 128
            seg = acs_g[:, ql:ql + 128] - row_g[:, ql:ql + 128]
            m = cb2 * jnp.exp(jnp.where(causal2, seg, -jnp.inf))
            xp = xd_g[:, ql:ql + 128].astype(BF16)
            x2 = jnp.concatenate([xp, xp], axis=0)
            x2 = jnp.where(blockdiag, x2, jnp.zeros_like(x2))
            yd.append(jnp.dot(m.astype(BF16), x2, preferred_element_type=F32))
        y_diag = jnp.concatenate(yd, axis=-1)

        xdw = (xd_g * jnp.exp(alast_g - acs_g)).astype(BF16)
        upd = jnp.dot(b_g.T.astype(BF16), xdw, preferred_element_type=F32)
        state_ref[g] = st * jnp.exp(alast_g) + upd

        y_g = y_diag + y_off + xs_g * dskip_ref[:, lo:lo + SSD_GROUP_W]
        y_g = y_g * _silu(z_ref[:, lo:lo + SSD_GROUP_W].astype(F32))
        ms = jnp.mean(y_g * y_g, axis=-1, keepdims=True)
        y_g = y_g * lax.rsqrt(ms + RMS_EPS) * nw_ref[:, lo:lo + SSD_GROUP_W]
        o_ref[:, lo:lo + SSD_GROUP_W] = y_g.astype(o_ref.dtype)


def ssd_core(zx, dt_raw, conv_w, conv_b, dt_bias, a_log, d_skip, norm_w, bsz, seq):
    t = zx.shape[0]
    nc = seq // CHUNK
    w = SSD_D_INNER
    pad = 128 - SSD_N_HEADS
    head = jnp.arange(128)
    expand = (head[:, None] == (jnp.arange(w)[None, :] // SSD_HEAD_DIM)).astype(F32)
    row = lambda v: v.reshape(1, -1).astype(F32)
    args = (zx, zx, zx, dt_raw,
            conv_w[:, :w], row(conv_b[:w]), conv_w[:, w:], row(conv_b[w:]),
            row(jnp.pad(dt_bias, (0, pad))), row(jnp.pad(a_log, (0, pad))),
            expand, row(jnp.repeat(d_skip, SSD_HEAD_DIM)), row(norm_w))
    const = lambda shape: pl.BlockSpec(shape, lambda b, c: (0, 0))
    return pl.pallas_call(
        _ssd_kernel,
        grid=(bsz, nc),
        in_specs=[pl.BlockSpec((CHUNK, w), lambda b, c: (b * nc + c, 0)),
                  pl.BlockSpec((CHUNK, w), lambda b, c: (b * nc + c, 1)),
                  pl.BlockSpec((CHUNK, w), lambda b, c: (b * nc + c, 2)),
                  pl.BlockSpec((CHUNK, 128), lambda b, c: (b * nc + c, 0)),
                  const((SSD_CONV, w)), const((1, w)), const((SSD_CONV, w)), const((1, w)),
                  const((1, 128)), const((1, 128)), const((128, w)), const((1, w)), const((1, w))],
        out_specs=pl.BlockSpec((CHUNK, w), lambda b, c: (b * nc + c, 0)),
        out_shape=jax.ShapeDtypeStruct((t, w), BF16),
        scratch_shapes=[pltpu.VMEM((SSD_N_GROUPS, SSD_D_STATE, SSD_GROUP_W), F32),
                        pltpu.VMEM((8, w), F32), pltpu.VMEM((8, w), F32),
                        pltpu.VMEM((8 + CHUNK, w), F32), pltpu.VMEM((8 + CHUNK, w), F32)],
        compiler_params=_cparams(("parallel", "arbitrary")),
        name="ssd_core",
    )(*args)


DA_VT_ROWS = 2 * DA_HEAD_DIM + 16


def _attn_kernel(q_ref, k_ref, vt_ref, lamp_ref, sw_ref, o_ref, m_ref, a_ref, s0_ref, s1_ref, b0_ref, b1_ref,
                 *, tq, tk, lambda_init):
    i = pl.program_id(2)
    hw = 2 * DA_HEAD_DIM
    q = q_ref[...]
    lane = lax.broadcasted_iota(I32, q.shape, 1)
    zero = jnp.zeros_like(q)
    qs = (jnp.where(lane < DA_HEAD_DIM, q, zero), jnp.where(lane >= DA_HEAD_DIM, q, zero))
    m_ref[...] = jnp.full_like(m_ref, -jnp.inf)
    a_ref[...] = jnp.zeros_like(a_ref)
    slots = ((s0_ref, b0_ref), (s1_ref, b1_ref))

    def qk(j, slot, masked):
        st_ref, mb_ref = slots[slot]
        kb = k_ref[pl.ds(pl.multiple_of(j * tk, tk), tk), :]
        if masked:
            kc = lax.broadcasted_iota(I32, (tk, tq), 0) // CHUNK
            qc = lax.broadcasted_iota(I32, (tk, tq), 1) // CHUNK
            visible = kc <= qc
        for mi in range(2):
            st = lax.dot_general(kb, qs[mi], NT_DIMS, preferred_element_type=F32)
            if masked:
                st = jnp.where(visible, st, -jnp.inf)
            st_ref[mi] = st
            mb_ref[mi] = jnp.max(st, axis=0, keepdims=True)

    def pv(j, slot):
        st_ref, mb_ref = slots[slot]
        vt = vt_ref[j]
        for mi in range(2):
            m_prev = m_ref[mi]
            m_next = jnp.maximum(m_prev, mb_ref[mi])
            p = jnp.exp((st_ref[mi] - m_next).astype(BF16))
            alpha = jnp.exp(m_prev - m_next)
            a_ref[mi] = alpha * a_ref[mi] + jnp.dot(vt, p, preferred_element_type=F32)
            m_ref[mi] = m_next

    @pl.when(i == 0)
    def _():
        qk(0, 0, True)
        pv(0, 0)

    @pl.when(i > 0)
    def _():
        qk(0, 0, False)
        n_pairs = lax.shift_right_logical(i - 1, 1)

        def body(jj, carry):
            j = 2 * jj
            qk(j + 1, 1, False)
            pv(j, 0)
            qk(j + 2, 0, False)
            pv(j + 1, 1)
            return carry

        lax.fori_loop(0, n_pairs, body, 0)

        @pl.when(i % 2 == 1)
        def _():
            qk(i, 1, True)
            pv(i - 1, 0)
            pv(i, 1)

        @pl.when(i % 2 == 0)
        def _():
            qk(i - 1, 1, False)
            pv(i - 2, 0)
            qk(i, 0, True)
            pv(i - 1, 1)
            pv(i, 0)

    lp = lamp_ref[...]
    lam = (jnp.exp(jnp.sum(lp[0:1] * lp[1:2], axis=-1, keepdims=True))
           - jnp.exp(jnp.sum(lp[2:3] * lp[3:4], axis=-1, keepdims=True)) + lambda_init)
    a0 = a_ref[0]
    a1 = a_ref[1]
    ot = a0[0:hw] / a0[hw:hw + 1] - lam * (a1[0:hw] / a1[hw:hw + 1])
    ms = jnp.mean(ot * ot, axis=0, keepdims=True)
    ot = ot * lax.rsqrt(ms + RMS_EPS) * (sw_ref[...] * (1.0 - lambda_init))
    o_ref[...] = ot.T.astype(o_ref.dtype)


def diff_attention_core(qkv, lam_params, subln_w, lambda_init, bsz, seq, *, tq=512):
    t = qkv.shape[0]
    tk = tq
    nq = seq // tq
    nh = DA_N_HEADS
    hw = 2 * DA_HEAD_DIM
    v = qkv[:, 2 * nh * hw:].reshape(bsz, nq, tk, nh, hw)
    vt = jnp.transpose(v, (0, 3, 1, 4, 2))
    vt = jnp.concatenate([vt, jnp.ones((bsz, nh, nq, DA_VT_ROWS - hw, tk), BF16)], axis=3)
    vt = vt.reshape(bsz * nh * nq, DA_VT_ROWS, tk)
    return pl.pallas_call(
        functools.partial(_attn_kernel, tq=tq, tk=tk, lambda_init=lambda_init),
        grid=(bsz, nh, nq),
        in_specs=[pl.BlockSpec((tq, hw), lambda b, h, i: (b * nq + i, h)),
                  pl.BlockSpec((seq, hw), lambda b, h, i: (b, nh + h)),
                  pl.BlockSpec((nq, DA_VT_ROWS, tk), lambda b, h, i: (b * nh + h, 0, 0)),
                  pl.BlockSpec((4, DA_HEAD_DIM), lambda b, h, i: (0, 0)),
                  pl.BlockSpec((hw, 1), lambda b, h, i: (0, 0))],
        out_specs=pl.BlockSpec((tq, hw), lambda b, h, i: (b * nq + i, h)),
        out_shape=jax.ShapeDtypeStruct((t, nh * hw), BF16),
        scratch_shapes=[pltpu.VMEM((2, 1, tq), F32), pltpu.VMEM((2, DA_VT_ROWS, tq), F32),
                        pltpu.VMEM((2, tk, tq), F32), pltpu.VMEM((2, tk, tq), F32),
                        pltpu.VMEM((2, 1, tq), F32), pltpu.VMEM((2, 1, tq), F32)],
        compiler_params=_cparams(("parallel", "parallel", "arbitrary")),
        name="diff_attention",
    )(qkv, qkv, vt, lam_params, subln_w.reshape(hw, 1))


def _candidate_blocks():
    kk = PEER_TOPK
    blocks = [(0, 1, 0, 8), (0, 1, 8, 8)]
    blocks += [(i, 1, 0, min(8, kk // (i + 1))) for i in range(1, 8)]
    blocks.append((8, 8, 0, 1))
    return blocks


def _peer_topk_kernel(q_ref, keys_ref, idx_ref, gate_ref, ts_ref, ti_ref, bs_ref, bp_ref, bi_ref, bg_ref, *, tt):
    kk = PEER_TOPK
    nk = PEER_N_KEYS
    big = jnp.float32(1e9)
    neg_inf = jnp.float32(-jnp.inf)
    key_iota = lax.broadcasted_iota(I32, (nk, tt), 0).astype(F32)
    r8 = lax.broadcasted_iota(I32, (8, tt), 0).astype(F32)
    blocks = _candidate_blocks()
    pos = jnp.concatenate(
        [jnp.where(r8 < max(n_i, n_j), (i0 * kk + j0) + r8 * (kk if n_i > 1 else 1), big)
         for i0, n_i, j0, n_j in blocks], axis=0)
    for h in range(PEER_HEADS):
        for c in range(2):
            hc = 2 * h + c
            s = lax.dot_general(keys_ref[hc], q_ref[:, hc * 128:(hc + 1) * 128], NT_DIMS,
                                preferred_element_type=F32)
            for k in range(kk):
                m = jnp.max(s, axis=0, keepdims=True)
                first = jnp.min(jnp.where(s == m, key_iota, big), axis=0, keepdims=True)
                s = jnp.where(key_iota == first, neg_inf, s)
                ts_ref[c, k:k + 1, :] = m
                ti_ref[c, k:k + 1, :] = first
        cand = []
        for i0, n_i, j0, n_j in blocks:
            a = ts_ref[0, i0:i0 + n_i, :]
            b = ts_ref[1, j0:j0 + 8, :] if n_j > 1 else ts_ref[1, j0:j0 + 1, :]
            cand.append(jnp.where(r8 < max(n_i, n_j), a + b, neg_inf))
        cand = jnp.concatenate(cand, axis=0)
        for k in range(kk):
            m = jnp.max(cand, axis=0, keepdims=True)
            first = jnp.min(jnp.where(cand == m, pos, big), axis=0, keepdims=True)
            cand = jnp.where(pos == first, neg_inf, cand)
            bs_ref[k:k + 1, :] = m
            bp_ref[k:k + 1, :] = first
        p = bp_ref[...]
        pi = jnp.floor(p * (1.0 / kk))
        pj = p - pi * kk
        e0 = jnp.zeros_like(p)
        e1 = jnp.zeros_like(p)
        for r in range(kk):
            e0 = e0 + jnp.where(pi == r, ti_ref[0, r:r + 1, :], 0.0)
            e1 = e1 + jnp.where(pj == r, ti_ref[1, r:r + 1, :], 0.0)
        rows = (e0 * nk + e1) * PEER_ROWS_PER_EXPERT
        bi_ref[h * kk:(h + 1) * kk, :] = rows.astype(I32)
        bs = bs_ref[...]
        e = jnp.exp(bs - bs[0:1])
        bg_ref[h * kk:(h + 1) * kk, :] = e / jnp.sum(e, axis=0, keepdims=True)
    idx_ref[...] = bi_ref[...].T
    gate_ref[...] = bg_ref[...].T


def peer_topk(q_bf16, keys_bf16, *, tt=128):
    t = q_bf16.shape[0]
    assert t % tt == 0
    return pl.pallas_call(
        functools.partial(_peer_topk_kernel, tt=tt),
        grid=(t // tt,),
        in_specs=[pl.BlockSpec((tt, q_bf16.shape[1]), lambda i: (i, 0)),
                  pl.BlockSpec(keys_bf16.shape, lambda i: (0, 0, 0))],
        out_specs=[pl.BlockSpec((tt, PEER_PICKS), lambda i: (i, 0)),
                   pl.BlockSpec((tt, PEER_PICKS), lambda i: (i, 0))],
        out_shape=[jax.ShapeDtypeStruct((t, PEER_PICKS), I32),
                   jax.ShapeDtypeStruct((t, PEER_PICKS), F32)],
        scratch_shapes=[pltpu.VMEM((2, PEER_TOPK, tt), F32), pltpu.VMEM((2, PEER_TOPK, tt), F32),
                        pltpu.VMEM((PEER_TOPK, tt), F32), pltpu.VMEM((PEER_TOPK, tt), F32),
                        pltpu.VMEM((PEER_PICKS, tt), I32), pltpu.VMEM((PEER_PICKS, tt), F32)],
        compiler_params=_cparams(("parallel",)),
        name="peer_topk",
    )(q_bf16, keys_bf16)


def pack_expert_table(tab):
    n, d = tab.shape
    tb = tab.astype(BF16).reshape(n, d // 256, 2, 128)
    lo = lax.bitcast_convert_type(tb[:, :, 0, :], jnp.uint16).astype(U32)
    hi = lax.bitcast_convert_type(tb[:, :, 1, :], jnp.uint16).astype(U32)
    return (lo | (hi << 16)).reshape(n * (d // 256), 128)


PEER_TOKEN_GROUP = 2


def _row_feed(rows_hbm, half_refs, sem, tab_ref, tt):
    step = pl.program_id(0)
    half = tt // 2
    n_half = half * PEER_PICKS

    def copy(s, h):
        start = pl.multiple_of((2 * s + h) * n_half, n_half)
        return pltpu.make_async_copy(rows_hbm.at[pl.ds(start, n_half)], half_refs[h], sem.at[h])

    @pl.when(step == 0)
    def _():
        copy(0, 0).start()

    copy(step, 0).wait()
    copy(step, 1).start()

    def gather(buf_refs, tok0):
        h = tok0 // half
        assert (tok0 + len(buf_refs) - 1) // half == h
        if tok0 == half:
            copy(step, 1).wait()

            @pl.when(step + 1 < pl.num_programs(0))
            def _():
                copy(step + 1, 0).start()

        r = PEER_ROWS_PER_EXPERT
        for k in range(PEER_PICKS):
            for g, buf_ref in enumerate(buf_refs):
                row = pl.multiple_of(half_refs[h][(tok0 + g - h * half) * PEER_PICKS + k], r)
                buf_ref[k * r:(k + 1) * r, :] = tab_ref[pl.ds(row, r), :]

    return gather


def _gathered(buf_ref):
    return pltpu.bitcast(buf_ref[...], BF16)


def _pipelined_tokens(tt, gather, compute, bufs):
    g = PEER_TOKEN_GROUP
    set_a, set_b = bufs[:g], bufs[g:]
    assert tt % (2 * g) == 0
    gather(set_b, 0)
    for t in range(0, tt, 2 * g):
        for n, buf in enumerate(set_b):
            compute(t + n, buf)
        gather(set_a, t + g)
        for n, buf in enumerate(set_a):
            compute(t + g + n, buf)
        if t + 2 * g < tt:
            gather(set_b, t + 2 * g)


def _segment_mask(rows, cols):
    sub = lax.broadcasted_iota(I32, (rows, cols), 0)
    lane = lax.broadcasted_iota(I32, (rows, cols), 1)
    return (lane % rows) == sub


def _peer_u_kernel(rows_hbm, hn_ref, gate_ref, tab_ref, fold_ref, w_ref, *scratch, tt):
    *bufs, r_ref, x8_ref, rows0_ref, rows1_ref, sem = scratch
    mask = _segment_mask(8, 8 * PEER_PICKS)
    for m in range(8):
        x8_ref[pl.ds(m, tt, stride=8), :] = hn_ref[:, m * 128:(m + 1) * 128].astype(F32)

    gather = _row_feed(rows_hbm, (rows0_ref, rows1_ref), sem, tab_ref, tt)

    def compute(tok, buf_ref):
        xm = x8_ref[pl.ds(pl.multiple_of(tok * 8, 8), 8), :].astype(BF16)
        y = lax.dot_general(xm, _gathered(buf_ref), NT_DIMS, preferred_element_type=F32)
        r_ref[pl.ds(tok, 1), :] = jnp.sum(jnp.where(mask, y, 0.0), axis=0, keepdims=True)

    _pipelined_tokens(tt, gather, compute, bufs)
    act = jnp.dot(r_ref[...], fold_ref[...], preferred_element_type=F32, precision=lax.Precision.HIGHEST)
    w_ref[...] = gate_ref[...] * (0.5 * act * (1.0 + lax.erf(act * (2.0 ** -0.5))))


def _peer_v_kernel(rows_hbm, w_ref, x_ref, tab_ref, spread_ref, o_ref, *scratch, tt):
    *bufs, wx_ref, o8_ref, rows0_ref, rows1_ref, sem = scratch
    mask = _segment_mask(8, 8 * PEER_PICKS)
    wx_ref[...] = jnp.dot(w_ref[...].astype(BF16), spread_ref[...], preferred_element_type=F32)

    gather = _row_feed(rows_hbm, (rows0_ref, rows1_ref), sem, tab_ref, tt)

    def compute(tok, buf_ref):
        wrow = jnp.broadcast_to(wx_ref[pl.ds(tok, 1), :], mask.shape)
        wsel = jnp.where(mask, wrow, 0.0).astype(BF16)
        rows = pl.ds(pl.multiple_of(tok * 8, 8), 8)
        o8_ref[rows, :] = jnp.dot(wsel, _gathered(buf_ref), preferred_element_type=F32)

    _pipelined_tokens(tt, gather, compute, bufs)
    for m in range(8):
        seg = slice(m * 128, (m + 1) * 128)
        o_ref[:, seg] = x_ref[:, seg] + o8_ref[pl.ds(m, tt, stride=8), :]


def _row_feed_scratch(tt):
    half = (tt // 2) * PEER_PICKS
    return [pltpu.SMEM((half,), I32), pltpu.SMEM((half,), I32), pltpu.SemaphoreType.DMA((2,))]


def _table_spec(shape):
    return pl.BlockSpec(shape, lambda i: (0, 0), pipeline_mode=pl.Buffered(1))


def peer_u(idx, hn, gate, utab, *, tt=64):
    t = gate.shape[0]
    assert t % tt == 0
    pick = jnp.arange(8 * PEER_PICKS) // 8
    fold = (pick[:, None] == jnp.arange(PEER_PICKS)[None, :]).astype(F32)
    gbuf = lambda: pltpu.VMEM((PEER_PICKS * PEER_ROWS_PER_EXPERT, 128), U32)
    return pl.pallas_call(
        functools.partial(_peer_u_kernel, tt=tt),
        grid=(t // tt,),
        in_specs=[pl.BlockSpec(memory_space=pl.ANY),
                  pl.BlockSpec((tt, hn.shape[1]), lambda i: (i, 0)),
                  pl.BlockSpec((tt, PEER_PICKS), lambda i: (i, 0)),
                  _table_spec(utab.shape),
                  pl.BlockSpec(fold.shape, lambda i: (0, 0))],
        out_specs=pl.BlockSpec((tt, PEER_PICKS), lambda i: (i, 0)),
        out_shape=jax.ShapeDtypeStruct((t, PEER_PICKS), F32),
        scratch_shapes=[gbuf() for _ in range(2 * PEER_TOKEN_GROUP)]
        + [pltpu.VMEM((tt, 8 * PEER_PICKS), F32), pltpu.VMEM((tt * 8, 128), F32)] + _row_feed_scratch(tt),
        compiler_params=_cparams(("arbitrary",)),
        name="peer_u",
    )(idx, hn, gate, utab, fold)


def peer_v(idx, w, x, vtab, *, tt=64):
    t = w.shape[0]
    assert t % tt == 0
    pick = jnp.arange(8 * PEER_PICKS) // 8
    spread = (jnp.arange(PEER_PICKS)[:, None] == pick[None, :]).astype(BF16)
    gbuf = lambda: pltpu.VMEM((PEER_PICKS * PEER_ROWS_PER_EXPERT, 128), U32)
    return pl.pallas_call(
        functools.partial(_peer_v_kernel, tt=tt),
        grid=(t // tt,),
        in_specs=[pl.BlockSpec(memory_space=pl.ANY),
                  pl.BlockSpec((tt, PEER_PICKS), lambda i: (i, 0)),
                  pl.BlockSpec((tt, x.shape[1]), lambda i: (i, 0)),
                  _table_spec(vtab.shape),
                  pl.BlockSpec(spread.shape, lambda i: (0, 0))],
        out_specs=pl.BlockSpec((tt, x.shape[1]), lambda i: (i, 0)),
        out_shape=jax.ShapeDtypeStruct(x.shape, F32),
        scratch_shapes=[gbuf() for _ in range(2 * PEER_TOKEN_GROUP)]
        + [pltpu.VMEM((tt, 8 * PEER_PICKS), F32), pltpu.VMEM((tt * 8, 128), F32)] + _row_feed_scratch(tt),
        compiler_params=_cparams(("arbitrary",)),
        name="peer_v",
    )(idx, w, x, vtab, spread)


def peer_block(x, norm_w, query_w, sub_keys, expert_u, expert_v):
    t, d = x.shape
    q, hn = norm_matmul(x, norm_w, query_w.astype(BF16), BF16, emit_norm=True)
    keys = sub_keys.reshape(PEER_HEADS * 2, PEER_N_KEYS, -1).astype(BF16)
    idx, gate = peer_topk(q, keys)
    idx = idx.reshape(t * PEER_PICKS)
    w = peer_u(idx, hn, gate, pack_expert_table(expert_u))
    return peer_v(idx, w, x, pack_expert_table(expert_v))


def ssd_layer(x, norm_w, in_w, conv_w, conv_b, dt_bias, a_log, d_skip, gnorm_w, out_w, bsz, seq):
    n_zx = SSD_D_INNER + conv_w.shape[1]
    zx = norm_matmul(x, norm_w, in_w[:, :n_zx].astype(BF16), BF16)
    w_dt = jnp.pad(in_w[:, n_zx:], ((0, 0), (0, 128 - SSD_N_HEADS))).astype(BF16)
    dt_raw = norm_matmul(x, norm_w, w_dt, F32)
    y = ssd_core(zx, dt_raw, conv_w, conv_b, dt_bias, a_log, d_skip, gnorm_w, bsz, seq)
    return matmul_res(y, out_w.astype(BF16), x)


def attn_layer(x, norm_w, qkv_w, q_norm_w, k_norm_w, lam_q1, lam_k1, lam_q2, lam_k2, subln_w, out_w,
               lambda_init, bsz, seq):
    d = x.shape[1]
    reps = d // DA_HEAD_DIM
    head_w = jnp.concatenate([jnp.tile(q_norm_w, reps) * (DA_HEAD_DIM ** -0.5), jnp.tile(k_norm_w, reps),
                              jnp.ones((d,), F32)])
    qkv = norm_matmul_qk(x, norm_w, qkv_w.astype(BF16), head_w, 2 * d)
    lam_params = jnp.stack([lam_q1, lam_k1, lam_q2, lam_k2]).astype(F32)
    o = diff_attention_core(qkv, lam_params, subln_w, lambda_init, bsz, seq)
    return matmul_res(o, out_w.astype(BF16), x)


def kernel(x, mix_norm_w, ffn_norm_w, ssd_in_w, ssd_conv_w, ssd_conv_b, ssd_dt_bias, ssd_a_log, ssd_d,
           ssd_norm_w, ssd_out_w, da_qkv_w, da_q_norm_w, da_k_norm_w, da_lam_q1, da_lam_k1, da_lam_q2,
           da_lam_k2, da_subln_w, da_out_w, peer_query_w, peer_sub_keys, peer_u, peer_v):
    bsz, seq, d = x.shape
    depth = mix_norm_w.shape[0]
    h = x.reshape(bsz * seq, d)
    for i in range(depth):
        j = i // 2
        if i % 2 == 0:
            h = ssd_layer(h, mix_norm_w[i], ssd_in_w[j], ssd_conv_w[j], ssd_conv_b[j], ssd_dt_bias[j],
                          ssd_a_log[j], ssd_d[j], ssd_norm_w[j], ssd_out_w[j], bsz, seq)
        else:
            lambda_init = 0.8 - 0.6 * math.exp(-0.3 * i)
            h = attn_layer(h, mix_norm_w[i], da_qkv_w[j], da_q_norm_w[j], da_k_norm_w[j], da_lam_q1[j],
                           da_lam_k1[j], da_lam_q2[j], da_lam_k2[j], da_subln_w[j], da_out_w[j],
                           lambda_init, bsz, seq)
        h = peer_block(h, ffn_norm_w[i], peer_query_w[i], peer_sub_keys[i], peer_u[i], peer_v[i])
    return h.reshape(bsz, seq, d)
```
